```python
import jax, jax.numpy as jnp
from jax import lax
import numpy as np

D_MODEL = 4096
BATCH = 4
SEQ = 2048
DEPTH = 2
DEC_BATCH = 8
DEC_SEQ = 4
PAST_LEN = 16384
PAGE_SIZE = 128

N_MIXERS = 2
N_ATTN_LAYERS = (DEPTH + 1) // 2
N_CONV_LAYERS = DEPTH // 2
N_HEADS = 32
HEAD_DIM = D_MODEL // N_HEADS
D_ATTN = N_HEADS * HEAD_DIM
D_CONV = D_MODEL
CONV_WIDTH = 31
CONV_STATE = CONV_WIDTH - 1
Q_BLOCK = 128
RMS_EPS = 1e-6
LN_EPS = 1e-5
NEG_INF = -1e30
ATTN_SCALE = HEAD_DIM ** -0.5
FORGET_BIAS_INIT = 3.0

kernel_name = 'fox_conformer_hybrid_decode_step'


def _rms_norm(x, g):
    xf = x.astype(jnp.float32)
    y = xf * lax.rsqrt(jnp.mean(xf * xf, axis=-1, keepdims=True) + RMS_EPS)
    return (y * g.astype(jnp.float32)).astype(x.dtype)


def _layer_norm(x, g, b):
    xf = x.astype(jnp.float32)
    mu = jnp.mean(xf, axis=-1, keepdims=True)
    var = jnp.mean(jnp.square(xf - mu), axis=-1, keepdims=True)
    y = (xf - mu) * lax.rsqrt(var + LN_EPS) * g.astype(jnp.float32) + b.astype(jnp.float32)
    return y.astype(x.dtype)


def _adaln_input(x, c, w_ada_i, b_ada_i, norm_g_i):
    mod = jax.nn.silu(c) @ w_ada_i + b_ada_i
    shift, scale, gate = jnp.split(mod, 3, axis=-1)
    h = _rms_norm(x, norm_g_i) * (1 + scale[:, None, :]) + shift[:, None, :]
    return h, gate[:, None, :]


def _fox_project(h, w_in, b_f, q_gain, k_gain):
    B, T, _ = h.shape
    proj = h @ w_in
    q, k, v, gate, f_logit = jnp.split(proj, [D_ATTN, 2 * D_ATTN, 3 * D_ATTN, 4 * D_ATTN], axis=-1)
    q = _rms_norm(q.reshape(B, T, N_HEADS, HEAD_DIM), q_gain)
    k = _rms_norm(k.reshape(B, T, N_HEADS, HEAD_DIM), k_gain)
    v = v.reshape(B, T, N_HEADS, HEAD_DIM)
    log_f = jax.nn.log_sigmoid(f_logit.astype(jnp.float32) + b_f.astype(jnp.float32))
    return q, k, v, gate, log_f


def _fox_prompt(q, k, v, log_f):
    B, T, H, Dh = q.shape
    cum = jnp.cumsum(log_f, axis=1)
    cum_k = jnp.transpose(cum, (0, 2, 1))[:, :, None, :]
    k_pos = jnp.arange(T)

    def block(start):
        qb = lax.dynamic_slice_in_dim(q, start, Q_BLOCK, axis=1)
        cq = lax.dynamic_slice_in_dim(cum, start, Q_BLOCK, axis=1)
        s = jnp.einsum('bqhd,bkhd->bhqk', qb, k, preferred_element_type=jnp.float32) * ATTN_SCALE
        s = s + jnp.transpose(cq, (0, 2, 1))[..., None] - cum_k
        q_pos = start + jnp.arange(Q_BLOCK)
        mask = k_pos[None, :] <= q_pos[:, None]
        p = jax.nn.softmax(jnp.where(mask, s, NEG_INF), axis=-1)
        return jnp.einsum('bhqk,bkhd->bqhd', p.astype(v.dtype), v)

    starts = jnp.arange(T // Q_BLOCK) * Q_BLOCK
    o = lax.map(block, starts)
    return jnp.transpose(o, (1, 0, 2, 3, 4)).reshape(B, T, H * Dh)


def _fox_sample(q, k, v, log_f, k_past, v_past, logf_past):
    DB, S, H, Dh = q.shape
    P = k_past.shape[1]
    cum_past = jnp.cumsum(logf_past.astype(jnp.float32), axis=1)
    cum_new = cum_past[:, -1:, :] + jnp.cumsum(log_f, axis=1)
    cq = jnp.transpose(cum_new, (0, 2, 1))[..., None]
    s_past = jnp.einsum('bqhd,bkhd->bhqk', q, k_past, preferred_element_type=jnp.float32) * ATTN_SCALE
    s_past = s_past + cq - jnp.transpose(cum_past, (0, 2, 1))[:, :, None, :]
    s_new = jnp.einsum('bqhd,bkhd->bhqk', q, k, preferred_element_type=jnp.float32) * ATTN_SCALE
    s_new = s_new + cq - jnp.transpose(cum_new, (0, 2, 1))[:, :, None, :]
    causal = jnp.arange(S)[None, :] <= jnp.arange(S)[:, None]
    s_new = jnp.where(causal, s_new, NEG_INF)
    p = jax.nn.softmax(jnp.concatenate([s_past, s_new], axis=-1), axis=-1)
    p_past, p_new = p[..., :P], p[..., P:]
    o = (jnp.einsum('bhqk,bkhd->bqhd', p_past.astype(v.dtype), v_past)
         + jnp.einsum('bhqk,bkhd->bqhd', p_new.astype(v.dtype), v))
    return o.reshape(DB, S, H * Dh)


def _conv_branch(h, conv_buf, w_in, dw_w, dw_b, ln_g, ln_b, w_out):
    proj = h @ w_in
    a, b, g = jnp.split(proj, 3, axis=-1)
    u = a * jax.nn.sigmoid(b)
    padded = jnp.concatenate([conv_buf.astype(u.dtype), u], axis=1)
    conv = lax.conv_general_dilated(
        padded, dw_w[:, None, :].astype(u.dtype), window_strides=(1,), padding='VALID',
        dimension_numbers=('NWC', 'WIO', 'NWC'), feature_group_count=D_CONV) + dw_b
    z = _layer_norm(conv, ln_g, ln_b)
    y = (jax.nn.silu(z) * jax.nn.silu(g)) @ w_out
    return y, padded[:, -CONV_STATE:, :]


def setup_inputs(seed: int = 0) -> dict:
    key = jax.random.key(seed)
    ks = jax.random.split(key, 24)
    f32 = jnp.float32
    n_pages = PAST_LEN // PAGE_SIZE
    n_used = DEC_BATCH * n_pages
    n_phys = n_used + n_used // 4
    nrm = lambda k, shape, s: jax.random.normal(k, shape, f32) * s
    page_table = jax.random.permutation(ks[0], n_phys)[:n_used].reshape(DEC_BATCH, n_pages).astype(jnp.int32)
    return {
        'x_prompt': nrm(ks[1], (BATCH, SEQ, D_MODEL), 1.0),
        'x_sample': nrm(ks[2], (DEC_BATCH, DEC_SEQ, D_MODEL), 1.0),
        'c_prompt': nrm(ks[3], (BATCH, D_MODEL), 1.0),
        'c_sample': nrm(ks[4], (DEC_BATCH, D_MODEL), 1.0),
        'cache_k': nrm(ks[5], (N_ATTN_LAYERS, n_phys, PAGE_SIZE, N_HEADS, HEAD_DIM), 1.0),
        'cache_v': nrm(ks[6], (N_ATTN_LAYERS, n_phys, PAGE_SIZE, N_HEADS, HEAD_DIM), 1.0),
        'cache_logf': jax.nn.log_sigmoid(FORGET_BIAS_INIT + nrm(ks[7], (N_ATTN_LAYERS, n_phys, PAGE_SIZE, N_HEADS), 1.0)),
        'state_conv': nrm(ks[8], (N_CONV_LAYERS, DEC_BATCH, CONV_STATE, D_CONV), 0.5),
        'page_table': page_table,
        'w_ada': nrm(ks[9], (DEPTH, D_MODEL, 3 * D_MODEL), 0.5 * D_MODEL ** -0.5),
        'b_ada': nrm(ks[10], (DEPTH, 3 * D_MODEL), 0.1),
        'norm_g': 1.0 + nrm(ks[11], (DEPTH, D_MODEL), 0.02),
        'w_in_attn': nrm(ks[12], (N_ATTN_LAYERS, D_MODEL, 4 * D_ATTN + N_HEADS), D_MODEL ** -0.5),
        'b_f': FORGET_BIAS_INIT + nrm(ks[13], (N_ATTN_LAYERS, N_HEADS), 0.5),
        'q_gain': 1.0 + nrm(ks[14], (N_ATTN_LAYERS, HEAD_DIM), 0.02),
        'k_gain': 1.0 + nrm(ks[15], (N_ATTN_LAYERS, HEAD_DIM), 0.02),
        'w_out_attn': nrm(ks[16], (N_ATTN_LAYERS, D_ATTN, D_MODEL), D_ATTN ** -0.5),
        'w_in_conv': nrm(ks[17], (N_CONV_LAYERS, D_MODEL, 3 * D_CONV), D_MODEL ** -0.5),
        'dw_w': nrm(ks[18], (N_CONV_LAYERS, CONV_WIDTH, D_CONV), CONV_WIDTH ** -0.5),
        'dw_b': nrm(ks[19], (N_CONV_LAYERS, D_CONV), 0.02),
        'ln_g': 1.0 + nrm(ks[20], (N_CONV_LAYERS, D_CONV), 0.02),
        'ln_b': nrm(ks[21], (N_CONV_LAYERS, D_CONV), 0.02),
        'w_out_conv': nrm(ks[22], (N_CONV_LAYERS, D_CONV, D_MODEL), D_CONV ** -0.5),
    }


def reference(x_prompt, x_sample, c_prompt, c_sample, cache_k, cache_v, cache_logf, state_conv,
              page_table, w_ada, b_ada, norm_g, w_in_attn, b_f, q_gain, k_gain, w_out_attn,
              w_in_conv, dw_w, dw_b, ln_g, ln_b, w_out_conv):
    DB, n_pages = page_table.shape
    xp, xs = x_prompt, x_sample
    kp_l, vp_l, fp_l, ks_l, vs_l, fs_l, cp_l, cs_l = [], [], [], [], [], [], [], []
    for i in range(DEPTH):
        j = i // N_MIXERS
        hp, gate_p = _adaln_input(xp, c_prompt, w_ada[i], b_ada[i], norm_g[i])
        hs, gate_s = _adaln_input(xs, c_sample, w_ada[i], b_ada[i], norm_g[i])
        if i % N_MIXERS == 0:
            q, k, v, g, lf = _fox_project(hp, w_in_attn[j], b_f[j], q_gain[j], k_gain[j])
            yp = (_fox_prompt(q, k, v, lf) * jax.nn.silu(g)) @ w_out_attn[j]
            kp_l.append(k); vp_l.append(v); fp_l.append(lf.astype(cache_logf.dtype))
            q, k, v, g, lf = _fox_project(hs, w_in_attn[j], b_f[j], q_gain[j], k_gain[j])
            k_past = cache_k[j][page_table].reshape(DB, n_pages * PAGE_SIZE, N_HEADS, HEAD_DIM)
            v_past = cache_v[j][page_table].reshape(DB, n_pages * PAGE_SIZE, N_HEADS, HEAD_DIM)
            f_past = cache_logf[j][page_table].reshape(DB, n_pages * PAGE_SIZE, N_HEADS)
            ys = (_fox_sample(q, k, v, lf, k_past, v_past, f_past) * jax.nn.silu(g)) @ w_out_attn[j]
            ks_l.append(k); vs_l.append(v); fs_l.append(lf.astype(cache_logf.dtype))
        else:
            zero_buf = jnp.zeros((xp.shape[0], CONV_STATE, D_CONV), xp.dtype)
            yp, buf_p = _conv_branch(hp, zero_buf, w_in_conv[j], dw_w[j], dw_b[j], ln_g[j], ln_b[j], w_out_conv[j])
            ys, buf_s = _conv_branch(hs, state_conv[j], w_in_conv[j], dw_w[j], dw_b[j], ln_g[j], ln_b[j], w_out_conv[j])
            cp_l.append(buf_p); cs_l.append(buf_s)
        xp = xp + gate_p * yp
        xs = xs + gate_s * ys
    return (xp, xs, jnp.stack(kp_l), jnp.stack(vp_l), jnp.stack(fp_l), jnp.stack(ks_l), jnp.stack(vs_l),
            jnp.stack(fs_l), jnp.stack(cp_l), jnp.stack(cs_l))
```

```python
import functools

import jax
import jax.numpy as jnp
from jax import lax
from jax.experimental import pallas as pl
from jax.experimental.pallas import tpu as pltpu

F32 = jnp.float32
BF16 = jnp.bfloat16
HIGHEST = lax.Precision.HIGHEST

RMS_EPS = 1e-6
LN_EPS = 1e-5
NEG_INF = -1e30

LANES = 128
SUBLANES = 8
VMEM_LIMIT_BYTES = 56 * 1024 * 1024

ROW_TILE = 1024
COL_TILE = 512
ATTN_TILE = 512
NORM_ROWS = 256
CONV_ROWS = 256
CONV_COLS = 512
CONV_CHUNK = 16
HIST_ROWS = 32
CUMSUM_CHUNK = 256

NT_DIMS = (((1,), (1,)), ((), ()))


def _params(*sem):
    return pltpu.CompilerParams(dimension_semantics=sem, vmem_limit_bytes=VMEM_LIMIT_BYTES)


def _silu(x):
    return x * jax.nn.sigmoid(x)


def _iota2(shape, dim):
    return lax.broadcasted_iota(jnp.int32, shape, dim)


def _ada_kernel(c_ref, w_ref, b_ref, o_ref):
    sc = _silu(c_ref[...]).astype(BF16)
    w = w_ref[0].astype(BF16)
    o_ref[0] = jnp.dot(sc, w, preferred_element_type=F32) + b_ref[0]


def _ada_mod(c, w_ada, b_ada):
    depth, d, n3 = w_ada.shape
    r = c.shape[0]
    tn = min(COL_TILE, n3)
    assert n3 % tn == 0 and r % SUBLANES == 0
    return pl.pallas_call(
        _ada_kernel,
        grid=(depth, n3 // tn),
        in_specs=[
            pl.BlockSpec((r, d), lambda i, n: (0, 0)),
            pl.BlockSpec((1, d, tn), lambda i, n: (i, 0, n)),
            pl.BlockSpec((1, 1, tn), lambda i, n: (i, 0, n)),
        ],
        out_specs=pl.BlockSpec((1, r, tn), lambda i, n: (i, 0, n)),
        out_shape=jax.ShapeDtypeStruct((depth, r, n3), F32),
        compiler_params=_params("arbitrary", "arbitrary"),
        name="ada_mod",
    )(c, w_ada, b_ada.reshape(depth, 1, n3))


def _norm_mod_kernel(x_ref, g_ref, sc_ref, sh_ref, o_ref):
    x = x_ref[0]
    ms = jnp.mean(x * x, axis=-1, keepdims=True)
    y = x * lax.rsqrt(ms + RMS_EPS) * g_ref[...]
    o_ref[0] = (y * (1.0 + sc_ref[0]) + sh_ref[0]).astype(o_ref.dtype)


def _norm_mod(x, g, scale, shift):
    b, t, d = x.shape
    tr = min(NORM_ROWS, t)
    assert t % tr == 0
    per_row = scale.shape[1] != 1
    mod_spec = (pl.BlockSpec((1, tr, d), lambda i, j: (i, j, 0)) if per_row
                else pl.BlockSpec((1, 1, d), lambda i, j: (i, 0, 0)))
    return pl.pallas_call(
        _norm_mod_kernel,
        grid=(b, t // tr),
        in_specs=[
            pl.BlockSpec((1, tr, d), lambda i, j: (i, j, 0)),
            pl.BlockSpec((1, d), lambda i, j: (0, 0)),
            mod_spec, mod_spec,
        ],
        out_specs=pl.BlockSpec((1, tr, d), lambda i, j: (i, j, 0)),
        out_shape=jax.ShapeDtypeStruct((b, t, d), BF16),
        compiler_params=_params("arbitrary", "arbitrary"),
        name="norm_mod",
    )(x, g.reshape(1, d), scale, shift)


def _proj_kernel(*refs, n_w, n_extra, epilogue):
    a = refs[0][0]
    accs = [jnp.dot(a, w[...], preferred_element_type=F32) for w in refs[1:1 + n_w]]
    epilogue(accs, refs[1 + n_w:1 + n_w + n_extra], refs[1 + n_w + n_extra:])


def _proj(a, w, col_offsets, n_cols, epilogue, extras, out_dtype, name):
    b, t, k = a.shape
    tm = min(ROW_TILE, t)
    tn = min(COL_TILE, n_cols)
    assert t % tm == 0 and n_cols % tn == 0 and all(off % tn == 0 for off in col_offsets)
    in_specs = [pl.BlockSpec((1, tm, k), lambda i, m, n: (i, m, 0))]
    for off in col_offsets:
        in_specs.append(pl.BlockSpec((k, tn), lambda i, m, n, ob=off // tn: (0, ob + n)))
    in_specs += [spec for _, spec in extras]
    return pl.pallas_call(
        functools.partial(_proj_kernel, n_w=len(col_offsets), n_extra=len(extras), epilogue=epilogue),
        grid=(b, t // tm, n_cols // tn),
        in_specs=in_specs,
        out_specs=pl.BlockSpec((1, tm, tn), lambda i, m, n: (i, m, n)),
        out_shape=jax.ShapeDtypeStruct((b, t, n_cols), out_dtype),
        compiler_params=_params("arbitrary", "arbitrary", "arbitrary"),
        name=name,
    )(a, *([w] * len(col_offsets)), *[arr for arr, _ in extras])


def _tile_spec(t, n_cols):
    return pl.BlockSpec((1, min(ROW_TILE, t), min(COL_TILE, n_cols)), lambda i, m, n: (i, m, n))


def _mod_spec(arr, t, n_cols):
    tn = min(COL_TILE, n_cols)
    if arr.shape[1] == 1:
        return pl.BlockSpec((1, 1, tn), lambda i, m, n: (i, 0, n))
    return pl.BlockSpec((1, min(ROW_TILE, t), tn), lambda i, m, n: (i, m, n))


def _epi_copy(accs, extra, outs):
    outs[0][0] = accs[0].astype(outs[0].dtype)


def _epi_silu(accs, extra, outs):
    outs[0][0] = _silu(accs[0]).astype(outs[0].dtype)


def _epi_glu(accs, extra, outs):
    outs[0][0] = (accs[0] * jax.nn.sigmoid(accs[1])).astype(outs[0].dtype)


def _epi_residual(accs, extra, outs):
    x_ref, gate_ref = extra
    outs[0][0] = x_ref[0] + gate_ref[0] * accs[0]


def _epi_log_sigmoid(accs, extra, outs):
    x = accs[0] + extra[0][...]
    outs[0][0] = jnp.minimum(x, 0.0) - jnp.log1p(jnp.exp(-jnp.abs(x)))


def _epi_head_norm(accs, extra, outs, *, scale):
    acc = accs[0]
    g = extra[0][...] * scale
    dh = g.shape[-1]
    for j in range(acc.shape[1] // dh):
        blk = acc[:, j * dh:(j + 1) * dh]
        ms = jnp.mean(blk * blk, axis=-1, keepdims=True)
        outs[0][0, :, j * dh:(j + 1) * dh] = (blk * lax.rsqrt(ms + RMS_EPS) * g).astype(outs[0].dtype)


def _cumsum_kernel(x_ref, o_ref):
    t = x_ref.shape[1]
    ch = min(CUMSUM_CHUNK, t)
    tri = (_iota2((ch, ch), 1) <= _iota2((ch, ch), 0)).astype(F32)
    carry = jnp.zeros((1, x_ref.shape[2]), F32)
    for i in range(t // ch):
        blk = x_ref[0, i * ch:(i + 1) * ch, :]
        cs = jnp.dot(tri, blk, precision=HIGHEST, preferred_element_type=F32) + carry
        o_ref[0, i * ch:(i + 1) * ch, :] = cs
        carry = cs[ch - 1:ch, :]


def _cumsum_t(x):
    b, t, n = x.shape
    return pl.pallas_call(
        _cumsum_kernel,
        grid=(b,),
        in_specs=[pl.BlockSpec((1, t, n), lambda i: (i, 0, 0))],
        out_specs=pl.BlockSpec((1, t, n), lambda i: (i, 0, 0)),
        out_shape=jax.ShapeDtypeStruct((b, t, n), F32),
        compiler_params=_params("arbitrary"),
        name="cumsum_t",
    )(x)


def _attn_kernel(q_ref, k_ref, v_ref, cq_ref, ck_ref, sg_ref, o_ref, *, tile):
    h = pl.program_id(1)
    i = pl.program_id(2)
    q = q_ref[0]
    lane = _iota2(cq_ref.shape[1:], 1)
    cq = jnp.sum(jnp.where(lane == h, cq_ref[0], 0.0), axis=-1, keepdims=True)

    def step(j, carry, masked):
        m, l, acc = carry
        start = pl.multiple_of(j * tile, tile)
        kb = k_ref[0, pl.ds(start, tile), :].astype(BF16)
        vb = v_ref[0, pl.ds(start, tile), :].astype(BF16)
        s = lax.dot_general(q, kb, NT_DIMS, preferred_element_type=F32)
        s = s + cq - ck_ref[0, 0, j]
        if masked:
            s = jnp.where(_iota2(s.shape, 1) <= _iota2(s.shape, 0), s, NEG_INF)
        m_new = jnp.maximum(m, jnp.max(s, axis=-1, keepdims=True))
        alpha = jnp.exp(m - m_new)
        p = jnp.exp(s - m_new)
        l = alpha * l + jnp.sum(p, axis=-1, keepdims=True)
        acc = alpha * acc + jnp.dot(p.astype(BF16), vb, preferred_element_type=F32)
        return m_new, l, acc

    init = (jnp.full((tile, 1), NEG_INF, F32), jnp.zeros((tile, 1), F32),
            jnp.zeros((tile, q.shape[1]), F32))
    carry = lax.fori_loop(0, i, lambda j, c: step(j, c, False), init)
    _, l, acc = step(i, carry, True)
    o_ref[0] = (acc / l * sg_ref[0]).astype(o_ref.dtype)


def _prompt_attention(q, k, v, cum, sg, n_heads):
    b, t, da = q.shape
    dh = da // n_heads
    tile = min(ATTN_TILE, t)
    assert t % tile == 0 and dh % LANES == 0
    cum_t = jnp.transpose(cum[:, :, :n_heads], (0, 2, 1)).reshape(b, n_heads, t // tile, 1, tile)
    return pl.pallas_call(
        functools.partial(_attn_kernel, tile=tile),
        grid=(b, n_heads, t // tile),
        in_specs=[
            pl.BlockSpec((1, tile, dh), lambda bi, h, i: (bi, i, h)),
            pl.BlockSpec((1, t, dh), lambda bi, h, i: (bi, 0, h)),
            pl.BlockSpec((1, t, dh), lambda bi, h, i: (bi, 0, h)),
            pl.BlockSpec((1, tile, cum.shape[2]), lambda bi, h, i: (bi, i, 0)),
            pl.BlockSpec((1, 1, t // tile, 1, tile), lambda bi, h, i: (bi, h, 0, 0, 0)),
            pl.BlockSpec((1, tile, dh), lambda bi, h, i: (bi, i, h)),
        ],
        out_specs=pl.BlockSpec((1, tile, dh), lambda bi, h, i: (bi, i, h)),
        out_shape=jax.ShapeDtypeStruct((b, t, da), BF16),
        compiler_params=_params("arbitrary", "arbitrary", "arbitrary"),
        name="prompt_attention",
    )(q, k, v, cum, cum_t, sg)


def _decode_kernel(pt_ref, wt_ref, k_ref, v_ref, lf_ref, kn_ref, vn_ref, lfc_ref, lfr_ref, sg_ref,
                   o_ref, m_scr, l_scr, acc_scr, carry_scr, colb_scr, p_scr, *, n_heads, n_new):
    p = pl.program_id(1)
    n_pages = pl.num_programs(1)
    hs = wt_ref.shape[1]
    n_rows = lf_ref.shape[1]
    tok_per_row = LANES // n_heads

    row_head = _iota2((hs, LANES), 0) // n_new
    lane_head = _iota2((hs, LANES), 1) % n_heads
    head_mask = jnp.where(row_head == lane_head, 0.0, NEG_INF)

    @pl.when(p == 0)
    def _init():
        m_scr[...] = jnp.full(m_scr.shape, NEG_INF, F32)
        l_scr[...] = jnp.zeros(l_scr.shape, F32)
        acc_scr[...] = jnp.zeros(acc_scr.shape, F32)
        carry_scr[...] = jnp.zeros(carry_scr.shape, F32)
        i0, i1 = _iota2((hs, hs), 0), _iota2((hs, hs), 1)
        tri = ((i0 // n_new == i1 // n_new) & (i1 % n_new <= i0 % n_new)).astype(F32)
        colb_scr[...] = jnp.dot(tri, lfc_ref[0], precision=HIGHEST, preferred_element_type=F32)

    def softmax_update(s_chunks, v_bf16):
        m_old = m_scr[...]
        mx = s_chunks[0]
        for sc in s_chunks[1:]:
            mx = jnp.maximum(mx, sc)
        m_new = jnp.maximum(m_old, jnp.max(mx, axis=-1, keepdims=True))
        alpha = jnp.exp(m_old - m_new)
        psum = None
        for c, sc in enumerate(s_chunks):
            pc = jnp.exp(sc - m_new[:, :sc.shape[1]])
            p_scr[:, c * LANES:c * LANES + sc.shape[1]] = pc.astype(BF16)
            psum = pc if psum is None else psum + pc
        width = (len(s_chunks) - 1) * LANES + s_chunks[-1].shape[1]
        m_scr[...] = m_new
        l_scr[...] = alpha * l_scr[...] + jnp.sum(psum, axis=-1, keepdims=True)
        pv = jnp.dot(p_scr[:, :width], v_bf16, preferred_element_type=F32)
        acc_scr[...] = alpha * acc_scr[...] + pv

    lf = lf_ref[0]
    l0, l1 = _iota2((LANES, LANES), 0), _iota2((LANES, LANES), 1)
    same_head = l0 % n_heads == l1 % n_heads
    later_in_row = (same_head & (l0 // n_heads > l1 // n_heads)).astype(F32)
    row_total = jnp.dot(lf, same_head.astype(F32), precision=HIGHEST, preferred_element_type=F32)
    later_rows = (_iota2((n_rows, n_rows), 1) > _iota2((n_rows, n_rows), 0)).astype(F32)
    suffix = (jnp.dot(lf, later_in_row, precision=HIGHEST, preferred_element_type=F32)
              + jnp.dot(later_rows, row_total, precision=HIGHEST, preferred_element_type=F32)
              + carry_scr[0:1, :])
    carry_scr[...] = carry_scr[...] + jnp.dot(jnp.ones((SUBLANES, n_rows), F32), row_total,
                                              precision=HIGHEST, preferred_element_type=F32)

    kb = k_ref[0].astype(BF16)
    r = lax.dot_general(wt_ref[0], kb, NT_DIMS, preferred_element_type=F32)
    bias = colb_scr[...] + head_mask
    chunks = [r[:, c * LANES:(c + 1) * LANES] + (suffix[c:c + 1, :] + bias) for c in range(n_rows)]
    softmax_update(chunks, v_ref[0].astype(BF16))

    @pl.when(p == n_pages - 1)
    def _finish():
        sh = kn_ref.shape[1]
        c0, c1 = _iota2((sh, sh), 0), _iota2((sh, sh), 1)
        tri = ((c0 % n_heads == c1 % n_heads) & (c0 // n_heads <= c1 // n_heads)).astype(F32)
        cum_row = jnp.dot(lfr_ref[0], tri, precision=HIGHEST, preferred_element_type=F32)
        rn = lax.dot_general(wt_ref[0], kn_ref[0].astype(BF16), NT_DIMS, preferred_element_type=F32)
        row = _iota2((hs, sh), 0)
        col = _iota2((hs, sh), 1)
        ok = (row // n_new == col % n_heads) & (col // n_heads <= row % n_new)
        sn = jnp.where(ok, rn + colb_scr[:, :sh] - cum_row[0:1, :], NEG_INF)
        softmax_update([sn], vn_ref[0].astype(BF16))
        o_ref[0] = (acc_scr[...] / l_scr[...] * sg_ref[0]).astype(o_ref.dtype)


def _decode_attention(page_table, wt, k_cache, v_cache, lf_cache, k_new, v_new, lf_col, lf_row, sg,
                      n_heads, n_new):
    db, n_pages = page_table.shape
    hs, dh = wt.shape[1:]
    n_phys, page_rows, _ = k_cache.shape
    lf_rows = lf_cache.shape[1]
    sh = k_new.shape[1]
    assert dh == LANES and hs <= LANES and LANES % n_heads == 0

    def page(bi, p, pt):
        return (pt[bi, n_pages - 1 - p], 0, 0)

    def seq(bi, p, pt):
        return (bi, 0, 0)

    grid_spec = pltpu.PrefetchScalarGridSpec(
        num_scalar_prefetch=1,
        grid=(db, n_pages),
        in_specs=[
            pl.BlockSpec((1, hs, dh), seq),
            pl.BlockSpec((1, page_rows, dh), page),
            pl.BlockSpec((1, page_rows, dh), page),
            pl.BlockSpec((1, lf_rows, LANES), page),
            pl.BlockSpec((1, sh, dh), seq),
            pl.BlockSpec((1, sh, dh), seq),
            pl.BlockSpec((1, hs, LANES), seq),
            pl.BlockSpec((1, SUBLANES, sh), seq),
            pl.BlockSpec((1, hs, dh), seq),
        ],
        out_specs=pl.BlockSpec((1, hs, dh), seq),
        scratch_shapes=[
            pltpu.VMEM((hs, LANES), F32),
            pltpu.VMEM((hs, LANES), F32),
            pltpu.VMEM((hs, dh), F32),
            pltpu.VMEM((SUBLANES, LANES), F32),
            pltpu.VMEM((hs, LANES), F32),
            pltpu.VMEM((hs, page_rows), BF16),
        ],
    )
    return pl.pallas_call(
        functools.partial(_decode_kernel, n_heads=n_heads, n_new=n_new),
        grid_spec=grid_spec,
        out_shape=jax.ShapeDtypeStruct((db, hs, dh), BF16),
        compiler_params=_params("arbitrary", "arbitrary"),
        name="decode_attention",
    )(page_table, wt, k_cache, v_cache, lf_cache, k_new, v_new, lf_col, lf_row, sg)


def _conv_kernel(u_ref, hist_ref, w_ref, b_ref, o_ref, win_ref, *, width, chunk):
    tt = u_ref.shape[1]
    pad = HIST_ROWS - (width - 1)

    @pl.when(pl.program_id(2) == 0)
    def _load_history():
        win_ref[0:HIST_ROWS, :] = hist_ref[0]

    win_ref[HIST_ROWS:HIST_ROWS + tt, :] = u_ref[0]
    bias = b_ref[...]
    for rc in range(tt // chunk):
        base = rc * chunk
        acc = jnp.broadcast_to(bias, (chunk, bias.shape[1]))
        for k in range(width):
            acc = acc + win_ref[base + pad + k:base + pad + k + chunk, :] * w_ref[k:k + 1, :]
        o_ref[0, base:base + chunk, :] = acc
    win_ref[0:HIST_ROWS, :] = win_ref[tt:tt + HIST_ROWS, :]


def _depthwise_conv(u, hist, w, bias):
    b, t, c = u.shape
    width = w.shape[0]
    tt = min(CONV_ROWS, t)
    tc = min(CONV_COLS, c)
    chunk = min(CONV_CHUNK, tt)
    assert t % tt == 0 and c % tc == 0 and tt % chunk == 0 and width - 1 <= HIST_ROWS
    w_pad = jnp.pad(w, ((0, HIST_ROWS - width), (0, 0)))
    return pl.pallas_call(
        functools.partial(_conv_kernel, width=width, chunk=chunk),
        grid=(b, c // tc, t // tt),
        in_specs=[
            pl.BlockSpec((1, tt, tc), lambda bi, ci, ti: (bi, ti, ci)),
            pl.BlockSpec((1, HIST_ROWS, tc), lambda bi, ci, ti: (bi, 0, ci)),
            pl.BlockSpec((HIST_ROWS, tc), lambda bi, ci, ti: (0, ci)),
            pl.BlockSpec((1, tc), lambda bi, ci, ti: (0, ci)),
        ],
        out_specs=pl.BlockSpec((1, tt, tc), lambda bi, ci, ti: (bi, ti, ci)),
        out_shape=jax.ShapeDtypeStruct((b, t, c), F32),
        scratch_shapes=[pltpu.VMEM((HIST_ROWS + tt, tc), F32)],
        compiler_params=_params("arbitrary", "arbitrary", "arbitrary"),
        name="depthwise_conv",
    )(u, hist, w_pad, bias.reshape(1, c))


def _ln_gate_kernel(x_ref, sg_ref, g_ref, b_ref, o_ref):
    x = x_ref[0]
    mu = jnp.mean(x, axis=-1, keepdims=True)
    xc = x - mu
    var = jnp.mean(xc * xc, axis=-1, keepdims=True)
    z = xc * lax.rsqrt(var + LN_EPS) * g_ref[...] + b_ref[...]
    o_ref[0] = (_silu(z) * sg_ref[0]).astype(o_ref.dtype)


def _ln_gate(x, sg, g, bias):
    b, t, c = x.shape
    tr = min(NORM_ROWS, t)
    assert t % tr == 0
    row = pl.BlockSpec((1, tr, c), lambda i, j: (i, j, 0))
    vec = pl.BlockSpec((1, c), lambda i, j: (0, 0))
    return pl.pallas_call(
        _ln_gate_kernel,
        grid=(b, t // tr),
        in_specs=[row, row, vec, vec],
        out_specs=row,
        out_shape=jax.ShapeDtypeStruct((b, t, c), BF16),
        compiler_params=_params("arbitrary", "arbitrary"),
        name="ln_gate",
    )(x, sg, g.reshape(1, c), bias.reshape(1, c))


def _pad_cols(x, n):
    return jnp.pad(x, ((0, 0), (0, n - x.shape[1])))


def kernel(x_prompt, x_sample, c_prompt, c_sample, cache_k, cache_v, cache_logf, state_conv,
           page_table, w_ada, b_ada, norm_g, w_in_attn, b_f, q_gain, k_gain, w_out_attn,
           w_in_conv, dw_w, dw_b, ln_g, ln_b, w_out_conv):
    b, t, d = x_prompt.shape
    db, s, _ = x_sample.shape
    depth = w_ada.shape[0]
    n_heads, dh = b_f.shape[1], q_gain.shape[1]
    da = n_heads * dh
    dc = dw_w.shape[2]
    conv_state = dw_w.shape[1] - 1
    page = cache_k.shape[2]
    n_phys = cache_k.shape[1]

    n_c = b + db
    c_all = jnp.pad(jnp.concatenate([c_prompt, c_sample], axis=0), ((0, -n_c % SUBLANES), (0, 0)))
    mod = _ada_mod(c_all, w_ada, b_ada)

    xp = x_prompt
    xs = x_sample.reshape(1, db * s, d)
    outs = {name: [] for name in ("kp", "vp", "fp", "ks", "vs", "fs", "cp", "cs")}

    for i in range(depth):
        j = i // 2
        mod_p = mod[i, :b][:, None, :]
        mod_s = jnp.repeat(mod[i, b:n_c], s, axis=0)[None]
        hp = _norm_mod(xp, norm_g[i], mod_p[..., d:2 * d], mod_p[..., :d])
        hs = _norm_mod(xs, norm_g[i], mod_s[..., d:2 * d], mod_s[..., :d])
        gate_p, gate_s = mod_p[..., 2 * d:], mod_s[..., 2 * d:]

        if i % 2 == 0:
            w_main = w_in_attn[j, :, :4 * da].astype(BF16)
            w_f = _pad_cols(w_in_attn[j, :, 4 * da:], LANES).astype(BF16)
            b_f_row = _pad_cols(b_f[j][None], LANES)
            w_out = w_out_attn[j].astype(BF16)
            gain_spec = pl.BlockSpec((1, dh), lambda bi, m, n: (0, 0))

            def project(h):
                q = _proj(h, w_main, [0], da, functools.partial(_epi_head_norm, scale=dh ** -0.5),
                          [(q_gain[j][None], gain_spec)], BF16, "proj_q")
                k = _proj(h, w_main, [da], da, functools.partial(_epi_head_norm, scale=1.0),
                          [(k_gain[j][None], gain_spec)], F32, "proj_k")
                v = _proj(h, w_main, [2 * da], da, _epi_copy, [], F32, "proj_v")
                sg = _proj(h, w_main, [3 * da], da, _epi_silu, [], BF16, "proj_gate")
                lf = _proj(h, w_f, [0], LANES, _epi_log_sigmoid,
                           [(b_f_row, pl.BlockSpec((1, LANES), lambda bi, m, n: (0, 0)))], F32, "proj_logf")
                return q, k, v, sg, lf

            q, k, v, sg, lf = project(hp)
            a = _prompt_attention(q, k, v, _cumsum_t(lf), sg, n_heads)
            outs["kp"].append(k.reshape(b, t, n_heads, dh))
            outs["vp"].append(v.reshape(b, t, n_heads, dh))
            outs["fp"].append(lf[:, :, :n_heads])
            xp = _proj(a, w_out, [0], d, _epi_residual,
                       [(xp, _tile_spec(t, d)), (gate_p, _mod_spec(gate_p, t, d))], F32, "out_attn")

            q, k, v, sg, lf = project(hs)
            lf_new = lf[0, :, :n_heads].reshape(db, s, n_heads)

            def rows_head_query(x):
                x = x.reshape(db, s, n_heads, dh)
                return jnp.transpose(x, (0, 2, 1, 3)).reshape(db, n_heads * s, dh)

            lf_col = jnp.transpose(lf_new, (0, 2, 1)).reshape(db, n_heads * s, 1)
            lf_col = jnp.broadcast_to(lf_col, (db, n_heads * s, LANES))
            lf_row = jnp.broadcast_to(lf_new.reshape(db, 1, s * n_heads), (db, SUBLANES, s * n_heads))
            a = _decode_attention(
                page_table, rows_head_query(q),
                cache_k[j].reshape(n_phys, page * n_heads, dh),
                cache_v[j].reshape(n_phys, page * n_heads, dh),
                cache_logf[j].reshape(n_phys, page * n_heads // LANES, LANES),
                k.reshape(db, s * n_heads, dh), v.reshape(db, s * n_heads, dh),
                lf_col, lf_row, rows_head_query(sg), n_heads, s)
            a = jnp.transpose(a.reshape(db, n_heads, s, dh), (0, 2, 1, 3)).reshape(1, db * s, da)
            outs["ks"].append(k.reshape(db, s, n_heads, dh))
            outs["vs"].append(v.reshape(db, s, n_heads, dh))
            outs["fs"].append(lf_new)
            xs = _proj(a, w_out, [0], d, _epi_residual,
                       [(xs, _tile_spec(db * s, d)), (gate_s, _mod_spec(gate_s, db * s, d))], F32, "out_attn")
        else:
            w_in = w_in_conv[j].astype(BF16)
            w_out = w_out_conv[j].astype(BF16)

            def conv_branch(h, hist):
                n_seq = hist.shape[0]
                rows = h.shape[0] * h.shape[1] // n_seq
                u = _proj(h, w_in, [0, dc], dc, _epi_glu, [], F32, "proj_glu").reshape(n_seq, rows, dc)
                sg = _proj(h, w_in, [2 * dc], dc, _epi_silu, [], BF16, "proj_gate").reshape(n_seq, rows, dc)
                pad_rows = -rows % SUBLANES
                u_in = jnp.pad(u, ((0, 0), (0, pad_rows), (0, 0)))
                sg = jnp.pad(sg, ((0, 0), (0, pad_rows), (0, 0)))
                hist = jnp.pad(hist, ((0, 0), (HIST_ROWS - conv_state, 0), (0, 0)))
                conv = _depthwise_conv(u_in, hist, dw_w[j], dw_b[j])
                a = _ln_gate(conv, sg, ln_g[j], ln_b[j])[:, :rows]
                return a.reshape(h.shape[0], h.shape[1], dc), u

            a, u = conv_branch(hp, jnp.zeros((b, conv_state, dc), F32))
            outs["cp"].append(u[:, t - conv_state:, :])
            xp = _proj(a, w_out, [0], d, _epi_residual,
                       [(xp, _tile_spec(t, d)), (gate_p, _mod_spec(gate_p, t, d))], F32, "out_conv")

            a, u = conv_branch(hs, state_conv[j])
            outs["cs"].append(jnp.concatenate([state_conv[j], u], axis=1)[:, -conv_state:, :])
            xs = _proj(a, w_out, [0], d, _epi_residual,
                       [(xs, _tile_spec(db * s, d)), (gate_s, _mod_spec(gate_s, db * s, d))], F32, "out_conv")

    return (xp, xs.reshape(db, s, d), jnp.stack(outs["kp"]), jnp.stack(outs["vp"]), jnp.stack(outs["fp"]),
            jnp.stack(outs["ks"]), jnp.stack(outs["vs"]), jnp.stack(outs["fs"]),
            jnp.stack(outs["cp"]), jnp.stack(outs["cs"]))
```

```python
import functools
import math

import jax
import jax.numpy as jnp
from jax import lax
from jax.experimental import pallas as pl
from jax.experimental.pallas import tpu as pltpu

F32 = jnp.float32
BF16 = jnp.bfloat16
HIGHEST = lax.Precision.HIGHEST

RMS_EPS = 1e-6
LN_EPS = 1e-5
NEG_INF = -1e30
LOG2E = math.log2(math.e)

LANES = 128
SUBLANES = 8
VMEM_LIMIT_BYTES = 56 * 1024 * 1024

ROW_TILE = 1024
COL_TILE = 512
ATTN_TILE = 512
NORM_ROWS = 256
CONV_ROWS = 256
CONV_COLS = 512
CONV_CHUNK = 32
HIST_ROWS = 32
CUMSUM_CHUNK = 256
BIAS_PAGES = 16
DECODE_PAGES = 2
ATTN_HALVES = 4

NT_DIMS = (((1,), (1,)), ((), ()))


def _params(*sem):
    return pltpu.CompilerParams(dimension_semantics=sem, vmem_limit_bytes=VMEM_LIMIT_BYTES)


def _silu(x):
    return x * jax.nn.sigmoid(x)


def _iota2(shape, dim):
    return lax.broadcasted_iota(jnp.int32, shape, dim)


def _ada_kernel(c_ref, w_ref, b_ref, o_ref):
    sc = _silu(c_ref[...]).astype(BF16)
    w = w_ref[0].astype(BF16)
    o_ref[0] = jnp.dot(sc, w, preferred_element_type=F32) + b_ref[0]


def _ada_mod(c, w_ada, b_ada):
    depth, d, n3 = w_ada.shape
    r = c.shape[0]
    tn = min(COL_TILE, n3)
    assert n3 % tn == 0 and r % SUBLANES == 0
    return pl.pallas_call(
        _ada_kernel,
        grid=(depth, n3 // tn),
        in_specs=[
            pl.BlockSpec((r, d), lambda i, n: (0, 0)),
            pl.BlockSpec((1, d, tn), lambda i, n: (i, 0, n)),
            pl.BlockSpec((1, 1, tn), lambda i, n: (i, 0, n)),
        ],
        out_specs=pl.BlockSpec((1, r, tn), lambda i, n: (i, 0, n)),
        out_shape=jax.ShapeDtypeStruct((depth, r, n3), F32),
        compiler_params=_params("arbitrary", "arbitrary"),
        name="ada_mod",
    )(c, w_ada, b_ada.reshape(depth, 1, n3))


def _norm_mod_kernel(x_ref, g_ref, sc_ref, sh_ref, o_ref):
    x = x_ref[0]
    ms = jnp.mean(x * x, axis=-1, keepdims=True)
    y = x * lax.rsqrt(ms + RMS_EPS) * g_ref[...]
    o_ref[0] = (y * (1.0 + sc_ref[0]) + sh_ref[0]).astype(o_ref.dtype)


def _norm_mod(x, g, scale, shift):
    b, t, d = x.shape
    tr = min(NORM_ROWS, t)
    assert t % tr == 0
    per_row = scale.shape[1] != 1
    mod_spec = (pl.BlockSpec((1, tr, d), lambda i, j: (i, j, 0)) if per_row
                else pl.BlockSpec((1, 1, d), lambda i, j: (i, 0, 0)))
    return pl.pallas_call(
        _norm_mod_kernel,
        grid=(b, t // tr),
        in_specs=[
            pl.BlockSpec((1, tr, d), lambda i, j: (i, j, 0)),
            pl.BlockSpec((1, d), lambda i, j: (0, 0)),
            mod_spec, mod_spec,
        ],
        out_specs=pl.BlockSpec((1, tr, d), lambda i, j: (i, j, 0)),
        out_shape=jax.ShapeDtypeStruct((b, t, d), BF16),
        compiler_params=_params("arbitrary", "arbitrary"),
        name="norm_mod",
    )(x, g.reshape(1, d), scale, shift)


def _proj_kernel(*refs, n_w, n_extras, epilogue):
    a_refs = refs[:2]
    w_refs = refs[2:2 + n_w]
    pos = 2 + n_w
    extras = (refs[pos:pos + n_extras], refs[pos + n_extras:pos + 2 * n_extras])
    pos += 2 * n_extras
    out_refs = refs[pos:pos + 2]
    w_scrs = refs[pos + 2:]
    first_rows = (pl.program_id(1) == 0) & (pl.program_id(2) == 0)

    def run(which):
        a = a_refs[which][0]
        accs = [jnp.dot(a, scr[...], preferred_element_type=F32) for scr in w_scrs]
        epilogue(accs, extras[which], out_refs[which])

    @pl.when(first_rows)
    def _new_columns():
        for w_ref, scr in zip(w_refs, w_scrs):
            scr[...] = w_ref[0].astype(BF16)
        run(1)

    run(0)


def _proj(a_p, a_s, w, layer, col_offsets, n_cols, epilogue, extras_p, extras_s, out_dtype, name,
          tn=COL_TILE):
    b, t, k = a_p.shape
    r = a_s.shape[1]
    tm = min(ROW_TILE, t)
    tn = min(tn, n_cols)
    assert t % tm == 0 and n_cols % tn == 0 and all(off % tn == 0 for off in col_offsets)
    assert len(extras_p) == len(extras_s)
    in_specs = [pl.BlockSpec((1, tm, k), lambda n, i, m: (i, m, 0)),
                pl.BlockSpec((1, r, k), lambda n, i, m: (0, 0, 0))]
    for off in col_offsets:
        in_specs.append(pl.BlockSpec((1, k, tn), lambda n, i, m, ob=off // tn: (layer, 0, ob + n)))
    in_specs += [spec for _, spec in extras_p] + [spec for _, spec in extras_s]
    return pl.pallas_call(
        functools.partial(_proj_kernel, n_w=len(col_offsets), n_extras=len(extras_p), epilogue=epilogue),
        grid=(n_cols // tn, b, t // tm),
        in_specs=in_specs,
        out_specs=[pl.BlockSpec((1, tm, tn), lambda n, i, m: (i, m, n)),
                   pl.BlockSpec((1, r, tn), lambda n, i, m: (0, 0, n))],
        out_shape=[jax.ShapeDtypeStruct((b, t, n_cols), out_dtype),
                   jax.ShapeDtypeStruct((1, r, n_cols), out_dtype)],
        scratch_shapes=[pltpu.VMEM((k, tn), BF16) for _ in col_offsets],
        compiler_params=_params("arbitrary", "arbitrary", "arbitrary"),
        name=name,
    )(a_p, a_s, *([w] * len(col_offsets)), *[arr for arr, _ in extras_p], *[arr for arr, _ in extras_s])


def _prompt_tile_spec(t, n_cols, tn=COL_TILE):
    return pl.BlockSpec((1, min(ROW_TILE, t), min(tn, n_cols)), lambda n, i, m: (i, m, n))


def _prompt_vec_spec(n_cols, tn=COL_TILE):
    return pl.BlockSpec((1, 1, min(tn, n_cols)), lambda n, i, m: (i, 0, n))


def _sample_tile_spec(r, n_cols, tn=COL_TILE):
    return pl.BlockSpec((1, r, min(tn, n_cols)), lambda n, i, m: (0, 0, n))


def _const_spec(shape):
    return pl.BlockSpec(shape, lambda n, i, m: (0,) * len(shape))


def _epi_copy(accs, extra, out):
    out[0] = accs[0].astype(out.dtype)


def _epi_silu(accs, extra, out):
    out[0] = _silu(accs[0]).astype(out.dtype)


def _epi_glu(accs, extra, out):
    out[0] = (accs[0] * jax.nn.sigmoid(accs[1])).astype(out.dtype)


def _epi_residual(accs, extra, out):
    x_ref, gate_ref = extra
    out[0] = x_ref[0] + gate_ref[0] * accs[0]


def _epi_log_sigmoid(accs, extra, out, *, n_valid):
    x = accs[0] + extra[0][...]
    y = jnp.minimum(x, 0.0) - jnp.log1p(jnp.exp(-jnp.abs(x)))
    out[0] = jnp.where(_iota2(y.shape, 1) < n_valid, y, 0.0)


def _epi_head_norm(accs, extra, out, *, scale):
    acc = accs[0]
    g = extra[0][...] * scale
    dh = g.shape[-1]
    for j in range(acc.shape[1] // dh):
        blk = acc[:, j * dh:(j + 1) * dh]
        ms = jnp.mean(blk * blk, axis=-1, keepdims=True)
        out[0, :, j * dh:(j + 1) * dh] = (blk * lax.rsqrt(ms + RMS_EPS) * g).astype(out.dtype)


def _cumsum_kernel(x_ref, o_ref):
    t = x_ref.shape[1]
    ch = min(CUMSUM_CHUNK, t)
    tri = (_iota2((ch, ch), 1) <= _iota2((ch, ch), 0)).astype(F32)
    carry = jnp.zeros((1, x_ref.shape[2]), F32)
    for i in range(t // ch):
        blk = x_ref[0, i * ch:(i + 1) * ch, :]
        cs = jnp.dot(tri, blk, precision=HIGHEST, preferred_element_type=F32) + carry
        o_ref[0, i * ch:(i + 1) * ch, :] = cs
        carry = cs[ch - 1:ch, :]


def _cumsum_t(x):
    b, t, n = x.shape
    return pl.pallas_call(
        _cumsum_kernel,
        grid=(b,),
        in_specs=[pl.BlockSpec((1, t, n), lambda i: (i, 0, 0))],
        out_specs=pl.BlockSpec((1, t, n), lambda i: (i, 0, 0)),
        out_shape=jax.ShapeDtypeStruct((b, t, n), F32),
        compiler_params=_params("arbitrary"),
        name="cumsum_t",
    )(x)


def _attn_kernel(q_ref, k_ref, v_ref, cq_ref, ck_ref, sg_ref, o_ref, kb_scr, vb_scr, m_scr, acc_scr,
                 *, tile, n_q, n_groups):
    h = pl.program_id(1)
    i = pl.program_id(2)
    dh = q_ref.shape[2]

    @pl.when(i == 0)
    def _cast_kv():
        kb_scr[...] = k_ref[0].astype(BF16)
        vb_scr[:, :dh] = v_ref[0].astype(BF16)
        vb_scr[:, dh:] = jnp.where(_iota2((vb_scr.shape[0], dh), 1) == 0, 1.0, 0.0).astype(BF16)

    rows = [slice(a * tile, (a + 1) * tile) for a in range(n_q)]
    qs = [q_ref[0, r, :] for r in rows]
    lane = _iota2((tile, cq_ref.shape[2]), 1)
    cqs = [jnp.sum(jnp.where(lane == h, cq_ref[0, r, :], 0.0), axis=-1, keepdims=True) * LOG2E for r in rows]

    m_scr[...] = jnp.full(m_scr.shape, NEG_INF, F32)
    acc_scr[...] = jnp.zeros(acc_scr.shape, F32)

    def step(a, j, masked):
        start = j * tile
        s = lax.dot_general(qs[a], kb_scr[pl.ds(start, tile), :], NT_DIMS, preferred_element_type=F32)
        s = s + cqs[a] - ck_ref[0, 0, j] * LOG2E
        if masked:
            s = jnp.where(_iota2(s.shape, 1) <= _iota2(s.shape, 0), s, NEG_INF)
        m_old = m_scr[a]
        m_new = jnp.maximum(m_old, jnp.max(s, axis=-1, keepdims=True))
        p = jnp.exp2(s - jnp.concatenate([m_new] * (tile // LANES), axis=1)).astype(BF16)
        pv = jnp.dot(p, vb_scr[pl.ds(start, tile), :], preferred_element_type=F32)
        alpha = jnp.exp2(m_old - m_new)
        acc_scr[a] = jnp.concatenate([alpha] * (2 * dh // LANES), axis=1) * acc_scr[a] + pv
        m_scr[a] = m_new

    def run_group(group):
        first = group * n_q
        for j in range(first):
            for a in range(n_q):
                step(a, j, False)
        for jj in range(n_q):
            for a in range(jj, n_q):
                step(a, first + jj, a == jj)
        for a, r in enumerate(rows):
            acc = acc_scr[a]
            o_ref[0, r, :] = (acc[:, :dh] / acc[:, dh:dh + 1] * sg_ref[0, r, :]).astype(o_ref.dtype)

    for group in range(n_groups):
        pl.when(i == group)(functools.partial(run_group, group))


def _prompt_attention(q, k, v, cum, sg, n_heads):
    b, t, da = q.shape
    dh = da // n_heads
    tile = min(ATTN_TILE, t)
    n_q = min(ATTN_HALVES, t // tile)
    group = n_q * tile
    assert t % group == 0 and dh % LANES == 0
    cum_t = jnp.transpose(cum[:, :, :n_heads], (0, 2, 1)).reshape(b, n_heads, t // tile, 1, tile)
    return pl.pallas_call(
        functools.partial(_attn_kernel, tile=tile, n_q=n_q, n_groups=t // group),
        grid=(b, n_heads, t // group),
        in_specs=[
            pl.BlockSpec((1, group, dh), lambda bi, h, i: (bi, i, h)),
            pl.BlockSpec((1, t, dh), lambda bi, h, i: (bi, 0, h)),
            pl.BlockSpec((1, t, dh), lambda bi, h, i: (bi, 0, h)),
            pl.BlockSpec((1, group, cum.shape[2]), lambda bi, h, i: (bi, i, 0)),
            pl.BlockSpec((1, 1, t // tile, 1, tile), lambda bi, h, i: (bi, h, 0, 0, 0)),
            pl.BlockSpec((1, group, dh), lambda bi, h, i: (bi, i, h)),
        ],
        out_specs=pl.BlockSpec((1, group, dh), lambda bi, h, i: (bi, i, h)),
        out_shape=jax.ShapeDtypeStruct((b, t, da), BF16),
        scratch_shapes=[pltpu.VMEM((t, dh), BF16), pltpu.VMEM((t, 2 * dh), BF16),
                        pltpu.VMEM((n_q, tile, LANES), F32), pltpu.VMEM((n_q, tile, 2 * dh), F32)],
        compiler_params=_params("arbitrary", "arbitrary", "arbitrary"),
        name="prompt_attention",
    )(q, k, v, cum, cum_t, sg)


def _decode_bias_kernel(pt_ref, *refs, n_heads):
    lf_refs, o_ref, carry_scr = refs[:-2], refs[-2], refs[-1]

    @pl.when(pl.program_id(1) == 0)
    def _init():
        carry_scr[...] = jnp.zeros(carry_scr.shape, F32)

    lf = jnp.concatenate([r[0] for r in lf_refs], axis=0)
    n = lf.shape[0]
    l0, l1 = _iota2((LANES, LANES), 0), _iota2((LANES, LANES), 1)
    same_head = l0 % n_heads == l1 % n_heads
    later_in_row = (same_head & (l0 // n_heads > l1 // n_heads)).astype(F32)
    row_total = jnp.dot(lf, same_head.astype(F32), precision=HIGHEST, preferred_element_type=F32)
    later_rows = (_iota2((n, n), 1) > _iota2((n, n), 0)).astype(F32)
    suffix = (jnp.dot(lf, later_in_row, precision=HIGHEST, preferred_element_type=F32)
              + jnp.dot(later_rows, row_total, precision=HIGHEST, preferred_element_type=F32)
              + carry_scr[0:1, :])
    carry_scr[...] = carry_scr[...] + jnp.dot(jnp.ones((SUBLANES, n), F32), row_total,
                                              precision=HIGHEST, preferred_element_type=F32)
    o_ref[0] = (suffix * LOG2E).reshape(o_ref.shape[1:])


def _decode_bias(page_table, lf_cache, n_heads):
    db, n_pages = page_table.shape
    rows = lf_cache.shape[1]
    pb = min(BIAS_PAGES, n_pages)
    groups = n_pages // pb
    assert n_pages % pb == 0

    def page_spec(k):
        return pl.BlockSpec((1, rows, LANES), lambda bi, g, pt: (pt[bi, (groups - 1 - g) * pb + k], 0, 0))

    grid_spec = pltpu.PrefetchScalarGridSpec(
        num_scalar_prefetch=1,
        grid=(db, groups),
        in_specs=[page_spec(k) for k in range(pb)],
        out_specs=pl.BlockSpec((1, pb, rows, LANES), lambda bi, g, pt: (bi, groups - 1 - g, 0, 0)),
        scratch_shapes=[pltpu.VMEM((SUBLANES, LANES), F32)],
    )
    return pl.pallas_call(
        functools.partial(_decode_bias_kernel, n_heads=n_heads),
        grid_spec=grid_spec,
        out_shape=jax.ShapeDtypeStruct((db, n_pages, rows, LANES), F32),
        compiler_params=_params("arbitrary", "arbitrary"),
        name="decode_bias",
    )(page_table, *([lf_cache] * pb))


def _decode_kernel(pt_ref, wt_ref, *refs, n_heads, n_new, n_par):
    page_refs = [refs[3 * i:3 * i + 3] for i in range(n_par)]
    kn_ref, vn_ref, lfc_ref, lfr_ref, sg_ref, o_ref, m_scr, l_scr, acc_scr, colb_scr, p_scr = refs[3 * n_par:]
    p = pl.program_id(1)
    n_steps = pl.num_programs(1)
    hs = wt_ref.shape[1]
    n_rows = page_refs[0][2].shape[2]

    @pl.when(p == 0)
    def _init():
        m_scr[...] = jnp.full(m_scr.shape, NEG_INF, F32)
        l_scr[...] = jnp.zeros(l_scr.shape, F32)
        acc_scr[...] = jnp.zeros(acc_scr.shape, F32)
        i0, i1 = _iota2((hs, hs), 0), _iota2((hs, hs), 1)
        tri = ((i0 // n_new == i1 // n_new) & (i1 % n_new <= i0 % n_new)).astype(F32)
        cum_col = jnp.dot(tri, lfc_ref[0], precision=HIGHEST, preferred_element_type=F32) * LOG2E
        other_head = _iota2((hs, LANES), 0) // n_new != _iota2((hs, LANES), 1) % n_heads
        colb_scr[0] = cum_col
        colb_scr[1] = jnp.where(other_head, NEG_INF, cum_col)

    def softmax_update(st, s_chunks, v_bf16):
        m_old = m_scr[st]
        mx = s_chunks[0]
        for sc in s_chunks[1:]:
            mx = jnp.maximum(mx, sc)
        m_new = jnp.maximum(m_old, jnp.max(mx, axis=-1, keepdims=True))
        alpha = jnp.exp2(m_old - m_new)
        psum = None
        for c, sc in enumerate(s_chunks):
            pc = jnp.exp2(sc - m_new[:, :sc.shape[1]])
            p_scr[st, :, c * LANES:c * LANES + sc.shape[1]] = pc.astype(BF16)
            psum = pc if psum is None else psum + pc
        width = (len(s_chunks) - 1) * LANES + s_chunks[-1].shape[1]
        m_scr[st] = m_new
        l_scr[st] = alpha * l_scr[st] + jnp.sum(psum, axis=-1, keepdims=True)
        pv = jnp.dot(p_scr[st, :, :width], v_bf16, preferred_element_type=F32)
        acc_scr[st] = alpha * acc_scr[st] + pv

    row_bias = colb_scr[1]
    for st, (k_ref, v_ref, bias_ref) in enumerate(page_refs):
        kb = k_ref[0].astype(BF16)
        r = lax.dot_general(wt_ref[0], kb, NT_DIMS, preferred_element_type=F32)
        bias = bias_ref[0, 0]
        chunks = [r[:, c * LANES:(c + 1) * LANES] + bias[c:c + 1, :] + row_bias for c in range(n_rows)]
        softmax_update(st, chunks, v_ref[0].astype(BF16))

    @pl.when(p == n_steps - 1)
    def _finish():
        sh = kn_ref.shape[1]
        c0, c1 = _iota2((sh, sh), 0), _iota2((sh, sh), 1)
        tri = ((c0 % n_heads == c1 % n_heads) & (c0 // n_heads <= c1 // n_heads)).astype(F32)
        cum_row = jnp.dot(lfr_ref[0], tri, precision=HIGHEST, preferred_element_type=F32) * LOG2E
        rn = lax.dot_general(wt_ref[0], kn_ref[0].astype(BF16), NT_DIMS, preferred_element_type=F32)
        row = _iota2((hs, sh), 0)
        col = _iota2((hs, sh), 1)
        ok = (row // n_new == col % n_heads) & (col // n_heads <= row % n_new)
        sn = jnp.where(ok, rn + colb_scr[0][:, :sh] - cum_row[0:1, :], NEG_INF)
        softmax_update(0, [sn], vn_ref[0].astype(BF16))
        m = m_scr[0]
        for st in range(1, n_par):
            m = jnp.maximum(m, m_scr[st])
        l = jnp.zeros(m.shape, F32)
        acc = jnp.zeros(acc_scr.shape[1:], F32)
        for st in range(n_par):
            w = jnp.exp2(m_scr[st] - m)
            l = l + w * l_scr[st]
            acc = acc + w * acc_scr[st]
        o_ref[0] = (acc / l * sg_ref[0]).astype(o_ref.dtype)


def _decode_attention(page_table, wt, k_cache, v_cache, bias, k_new, v_new, lf_col, lf_row, sg,
                      n_heads, n_new):
    db, n_pages = page_table.shape
    hs, dh = wt.shape[1:]
    n_phys, page_rows, _ = k_cache.shape
    bias_rows = bias.shape[2]
    sh = k_new.shape[1]
    n_par = DECODE_PAGES
    assert dh == LANES and hs <= LANES and LANES % n_heads == 0 and n_pages % n_par == 0

    def seq(bi, p, pt):
        return (bi, 0, 0)

    page_specs = []
    for i in range(n_par):
        def page(bi, p, pt, i=i):
            return (pt[bi, p * n_par + i], 0, 0)
        page_specs += [pl.BlockSpec((1, page_rows, dh), page), pl.BlockSpec((1, page_rows, dh), page),
                       pl.BlockSpec((1, 1, bias_rows, LANES), lambda bi, p, pt, i=i: (bi, p * n_par + i, 0, 0))]

    grid_spec = pltpu.PrefetchScalarGridSpec(
        num_scalar_prefetch=1,
        grid=(db, n_pages // n_par),
        in_specs=[pl.BlockSpec((1, hs, dh), seq)] + page_specs + [
            pl.BlockSpec((1, sh, dh), seq),
            pl.BlockSpec((1, sh, dh), seq),
            pl.BlockSpec((1, hs, LANES), seq),
            pl.BlockSpec((1, SUBLANES, sh), seq),
            pl.BlockSpec((1, hs, dh), seq),
        ],
        out_specs=pl.BlockSpec((1, hs, dh), seq),
        scratch_shapes=[
            pltpu.VMEM((n_par, hs, LANES), F32),
            pltpu.VMEM((n_par, hs, LANES), F32),
            pltpu.VMEM((n_par, hs, dh), F32),
            pltpu.VMEM((2, hs, LANES), F32),
            pltpu.VMEM((n_par, hs, page_rows), BF16),
        ],
    )
    return pl.pallas_call(
        functools.partial(_decode_kernel, n_heads=n_heads, n_new=n_new, n_par=n_par),
        grid_spec=grid_spec,
        out_shape=jax.ShapeDtypeStruct((db, hs, dh), BF16),
        compiler_params=_params("arbitrary", "arbitrary"),
        name="decode_attention",
    )(page_table, wt, *([k_cache, v_cache, bias] * n_par), k_new, v_new, lf_col, lf_row, sg)


def _conv_kernel(u_ref, hist_ref, w_ref, b_ref, o_ref, win_ref, shift_ref, *, width, chunk):
    tt = u_ref.shape[1]
    n = HIST_ROWS + tt
    pad = HIST_ROWS - (width - 1)

    @pl.when(pl.program_id(2) == 0)
    def _load_history():
        win_ref[0:HIST_ROWS, :] = hist_ref[0]

    win_ref[HIST_ROWS:n, :] = u_ref[0]
    for r in range(1, SUBLANES):
        shift_ref[r, 0:n - SUBLANES, :] = win_ref[r:r + n - SUBLANES, :]
    bias = b_ref[...]
    for rc in range(tt // chunk):
        base = rc * chunk
        acc = jnp.broadcast_to(bias, (chunk, bias.shape[1]))
        for k in range(width):
            a, r = divmod(pad + k, SUBLANES)
            lo = base + a * SUBLANES
            rows = win_ref[lo:lo + chunk, :] if r == 0 else shift_ref[r, lo:lo + chunk, :]
            acc = acc + rows * w_ref[k:k + 1, :]
        o_ref[0, base:base + chunk, :] = acc
    win_ref[0:HIST_ROWS, :] = win_ref[tt:n, :]


def _depthwise_conv(u, hist, w, bias):
    b, t, c = u.shape
    width = w.shape[0]
    tt = min(CONV_ROWS, t)
    tc = min(CONV_COLS, c)
    chunk = min(CONV_CHUNK, tt)
    assert t % tt == 0 and c % tc == 0 and tt % chunk == 0 and width - 1 <= HIST_ROWS
    w_pad = jnp.pad(w, ((0, HIST_ROWS - width), (0, 0)))
    return pl.pallas_call(
        functools.partial(_conv_kernel, width=width, chunk=chunk),
        grid=(b, c // tc, t // tt),
        in_specs=[
            pl.BlockSpec((1, tt, tc), lambda bi, ci, ti: (bi, ti, ci)),
            pl.BlockSpec((1, HIST_ROWS, tc), lambda bi, ci, ti: (bi, 0, ci)),
            pl.BlockSpec((HIST_ROWS, tc), lambda bi, ci, ti: (0, ci)),
            pl.BlockSpec((1, tc), lambda bi, ci, ti: (0, ci)),
        ],
        out_specs=pl.BlockSpec((1, tt, tc), lambda bi, ci, ti: (bi, ti, ci)),
        out_shape=jax.ShapeDtypeStruct((b, t, c), F32),
        scratch_shapes=[pltpu.VMEM((HIST_ROWS + tt, tc), F32),
                        pltpu.VMEM((SUBLANES, HIST_ROWS + tt, tc), F32)],
        compiler_params=_params("arbitrary", "arbitrary", "arbitrary"),
        name="depthwise_conv",
    )(u, hist, w_pad, bias.reshape(1, c))


def _ln_gate_kernel(x_ref, sg_ref, g_ref, b_ref, o_ref):
    x = x_ref[0]
    mu = jnp.mean(x, axis=-1, keepdims=True)
    xc = x - mu
    var = jnp.mean(xc * xc, axis=-1, keepdims=True)
    z = xc * lax.rsqrt(var + LN_EPS) * g_ref[...] + b_ref[...]
    o_ref[0] = (_silu(z) * sg_ref[0]).astype(o_ref.dtype)


def _ln_gate(x, sg, g, bias):
    b, t, c = x.shape
    tr = min(NORM_ROWS, t)
    assert t % tr == 0
    row = pl.BlockSpec((1, tr, c), lambda i, j: (i, j, 0))
    vec = pl.BlockSpec((1, c), lambda i, j: (0, 0))
    return pl.pallas_call(
        _ln_gate_kernel,
        grid=(b, t // tr),
        in_specs=[row, row, vec, vec],
        out_specs=row,
        out_shape=jax.ShapeDtypeStruct((b, t, c), BF16),
        compiler_params=_params("arbitrary", "arbitrary"),
        name="ln_gate",
    )(x, sg, g.reshape(1, c), bias.reshape(1, c))


def kernel(x_prompt, x_sample, c_prompt, c_sample, cache_k, cache_v, cache_logf, state_conv,
           page_table, w_ada, b_ada, norm_g, w_in_attn, b_f, q_gain, k_gain, w_out_attn,
           w_in_conv, dw_w, dw_b, ln_g, ln_b, w_out_conv):
    b, t, d = x_prompt.shape
    db, s, _ = x_sample.shape
    rs = db * s
    depth = w_ada.shape[0]
    n_heads, dh = b_f.shape[1], q_gain.shape[1]
    da = n_heads * dh
    dc = dw_w.shape[2]
    conv_state = dw_w.shape[1] - 1
    page = cache_k.shape[2]
    n_phys = cache_k.shape[1]

    n_c = b + db
    c_all = jnp.pad(jnp.concatenate([c_prompt, c_sample], axis=0), ((0, -n_c % SUBLANES), (0, 0)))
    mod = _ada_mod(c_all, w_ada, b_ada)

    xp = x_prompt
    xs = x_sample.reshape(1, rs, d)
    outs = {name: [] for name in ("kp", "vp", "fp", "ks", "vs", "fs", "cp", "cs")}

    def residual_out(a_p, a_s, w, j, xp, xs, gate_p, gate_s, name):
        return _proj(a_p, a_s, w, j, [0], d, _epi_residual,
                     [(xp, _prompt_tile_spec(t, d)), (gate_p, _prompt_vec_spec(d))],
                     [(xs, _sample_tile_spec(rs, d)), (gate_s, _sample_tile_spec(rs, d))], F32, name)

    for i in range(depth):
        j = i // 2
        mod_p = mod[i, :b][:, None, :]
        mod_s = jnp.repeat(mod[i, b:n_c], s, axis=0)[None]
        hp = _norm_mod(xp, norm_g[i], mod_p[..., d:2 * d], mod_p[..., :d])
        hs = _norm_mod(xs, norm_g[i], mod_s[..., d:2 * d], mod_s[..., :d])
        gate_p, gate_s = mod_p[..., 2 * d:], mod_s[..., 2 * d:]

        if i % 2 == 0:
            q_extra = [(q_gain[j][None], _const_spec((1, dh)))]
            k_extra = [(k_gain[j][None], _const_spec((1, dh)))]
            q_p, q_s = _proj(hp, hs, w_in_attn, j, [0], da,
                             functools.partial(_epi_head_norm, scale=dh ** -0.5 * LOG2E),
                             q_extra, q_extra, BF16, "proj_q")
            k_p, k_s = _proj(hp, hs, w_in_attn, j, [da], da, functools.partial(_epi_head_norm, scale=1.0),
                             k_extra, k_extra, F32, "proj_k")
            v_p, v_s = _proj(hp, hs, w_in_attn, j, [2 * da], da, _epi_copy, [], [], F32, "proj_v")
            sg_p, sg_s = _proj(hp, hs, w_in_attn, j, [3 * da], da, _epi_silu, [], [], BF16, "proj_gate")
            b_f_row = jnp.pad(b_f[j][None], ((0, 0), (0, LANES - n_heads)))
            bf_extra = [(b_f_row, _const_spec((1, LANES)))]
            lf_p, lf_s = _proj(hp, hs, w_in_attn, j, [4 * da], LANES,
                               functools.partial(_epi_log_sigmoid, n_valid=n_heads),
                               bf_extra, bf_extra, F32, "proj_logf", tn=LANES)

            a_p = _prompt_attention(q_p, k_p, v_p, _cumsum_t(lf_p), sg_p, n_heads)
            outs["kp"].append(k_p.reshape(b, t, n_heads, dh))
            outs["vp"].append(v_p.reshape(b, t, n_heads, dh))
            outs["fp"].append(lf_p[:, :, :n_heads])

            lf_new = lf_s[0, :, :n_heads].reshape(db, s, n_heads)

            def rows_head_query(x):
                x = x.reshape(db, s, n_heads, dh)
                return jnp.transpose(x, (0, 2, 1, 3)).reshape(db, n_heads * s, dh)

            lf_col = jnp.transpose(lf_new, (0, 2, 1)).reshape(db, n_heads * s, 1)
            lf_col = jnp.broadcast_to(lf_col, (db, n_heads * s, LANES))
            lf_row = jnp.broadcast_to(lf_new.reshape(db, 1, s * n_heads), (db, SUBLANES, s * n_heads))
            bias = _decode_bias(page_table, cache_logf[j].reshape(n_phys, page * n_heads // LANES, LANES), n_heads)
            a_s = _decode_attention(
                page_table, rows_head_query(q_s),
                cache_k[j].reshape(n_phys, page * n_heads, dh),
                cache_v[j].reshape(n_phys, page * n_heads, dh),
                bias, k_s.reshape(db, s * n_heads, dh), v_s.reshape(db, s * n_heads, dh),
                lf_col, lf_row, rows_head_query(sg_s), n_heads, s)
            a_s = jnp.transpose(a_s.reshape(db, n_heads, s, dh), (0, 2, 1, 3)).reshape(1, rs, da)
            outs["ks"].append(k_s.reshape(db, s, n_heads, dh))
            outs["vs"].append(v_s.reshape(db, s, n_heads, dh))
            outs["fs"].append(lf_new)

            xp, xs = residual_out(a_p, a_s, w_out_attn, j, xp, xs, gate_p, gate_s, "out_attn")
        else:
            u_p, u_s = _proj(hp, hs, w_in_conv, j, [0, dc], dc, _epi_glu, [], [], F32, "proj_glu", tn=COL_TILE // 2)
            sg_p, sg_s = _proj(hp, hs, w_in_conv, j, [2 * dc], dc, _epi_silu, [], [], BF16, "proj_gate")

            def conv_branch(u, sg, hist):
                rows = u.shape[1]
                pad_rows = -rows % SUBLANES
                u = jnp.pad(u, ((0, 0), (0, pad_rows), (0, 0)))
                sg = jnp.pad(sg, ((0, 0), (0, pad_rows), (0, 0)))
                hist = jnp.pad(hist, ((0, 0), (HIST_ROWS - conv_state, 0), (0, 0)))
                conv = _depthwise_conv(u, hist, dw_w[j], dw_b[j])
                return _ln_gate(conv, sg, ln_g[j], ln_b[j])[:, :rows]

            a_p = conv_branch(u_p, sg_p, jnp.zeros((b, conv_state, dc), F32))
            outs["cp"].append(u_p[:, t - conv_state:, :])
            u_s = u_s.reshape(db, s, dc)
            a_s = conv_branch(u_s, sg_s.reshape(db, s, dc), state_conv[j]).reshape(1, rs, dc)
            outs["cs"].append(jnp.concatenate([state_conv[j], u_s], axis=1)[:, -conv_state:, :])

            xp, xs = residual_out(a_p, a_s, w_out_conv, j, xp, xs, gate_p, gate_s, "out_conv")

    return (xp, xs.reshape(db, s, d), jnp.stack(outs["kp"]), jnp.stack(outs["vp"]), jnp.stack(outs["fp"]),
            jnp.stack(outs["ks"]), jnp.stack(outs["vs"]), jnp.stack(outs["fs"]),
            jnp.stack(outs["cp"]), jnp.stack(outs["cs"]))
```

```python
import functools
import math

import jax
import jax.numpy as jnp
from jax import lax
from jax.experimental import pallas as pl
from jax.experimental.pallas import tpu as pltpu

F32 = jnp.float32
BF16 = jnp.bfloat16
HIGHEST = lax.Precision.HIGHEST

RMS_EPS = 1e-6
LN_EPS = 1e-5
NEG_INF = -1e30
LOG2E = math.log2(math.e)

LANES = 128
SUBLANES = 8
VMEM_LIMIT_BYTES = 63 * 1024 * 1024

ROW_TILE = 512
COL_TILE = 1024
ADA_COLS = 512
ATTN_TILE = 512
NORM_ROWS = 256
CONV_ROWS = 256
CONV_COLS = 512
CONV_CHUNK = 32
HIST_ROWS = 32
CUMSUM_CHUNK = 256
BIAS_PAGES = 16
DECODE_PAGES = 4
ATTN_HALVES = 4

NT_DIMS = (((1,), (1,)), ((), ()))


def _params(*sem):
    return pltpu.CompilerParams(dimension_semantics=sem, vmem_limit_bytes=VMEM_LIMIT_BYTES)


def _silu(x):
    return x * jax.nn.sigmoid(x)


def _iota2(shape, dim):
    return lax.broadcasted_iota(jnp.int32, shape, dim)


def _ada_kernel(c_ref, w_ref, b_ref, o_ref):
    sc = _silu(c_ref[...]).astype(BF16)
    w = w_ref[0].astype(BF16)
    o_ref[0] = jnp.dot(sc, w, preferred_element_type=F32) + b_ref[0]


def _ada_mod(c, w_ada, b_ada):
    depth, d, n3 = w_ada.shape
    r = c.shape[0]
    tn = min(ADA_COLS, n3)
    assert n3 % tn == 0 and r % SUBLANES == 0
    return pl.pallas_call(
        _ada_kernel,
        grid=(depth, n3 // tn),
        in_specs=[
            pl.BlockSpec((r, d), lambda i, n: (0, 0)),
            pl.BlockSpec((1, d, tn), lambda i, n: (i, 0, n)),
            pl.BlockSpec((1, 1, tn), lambda i, n: (i, 0, n)),
        ],
        out_specs=pl.BlockSpec((1, r, tn), lambda i, n: (i, 0, n)),
        out_shape=jax.ShapeDtypeStruct((depth, r, n3), F32),
        compiler_params=_params("arbitrary", "arbitrary"),
        name="ada_mod",
    )(c, w_ada, b_ada.reshape(depth, 1, n3))


def _norm_mod_kernel(x_ref, g_ref, sc_ref, sh_ref, o_ref):
    x = x_ref[0]
    ms = jnp.mean(x * x, axis=-1, keepdims=True)
    y = x * lax.rsqrt(ms + RMS_EPS) * g_ref[...]
    o_ref[0] = (y * (1.0 + sc_ref[0]) + sh_ref[0]).astype(o_ref.dtype)


def _norm_mod(x, g, scale, shift):
    b, t, d = x.shape
    tr = min(NORM_ROWS, t)
    assert t % tr == 0
    per_row = scale.shape[1] != 1
    mod_spec = (pl.BlockSpec((1, tr, d), lambda i, j: (i, j, 0)) if per_row
                else pl.BlockSpec((1, 1, d), lambda i, j: (i, 0, 0)))
    return pl.pallas_call(
        _norm_mod_kernel,
        grid=(b, t // tr),
        in_specs=[
            pl.BlockSpec((1, tr, d), lambda i, j: (i, j, 0)),
            pl.BlockSpec((1, d), lambda i, j: (0, 0)),
            mod_spec, mod_spec,
        ],
        out_specs=pl.BlockSpec((1, tr, d), lambda i, j: (i, j, 0)),
        out_shape=jax.ShapeDtypeStruct((b, t, d), BF16),
        compiler_params=_params("arbitrary", "arbitrary"),
        name="norm_mod",
    )(x, g.reshape(1, d), scale, shift)


def _proj_kernel(*refs, n_w, n_extras, n_outs, epilogues, transposed):
    a_refs = refs[:2]
    w_refs = refs[2:2 + n_w]
    pos = 2 + n_w
    extras = (refs[pos:pos + n_extras], refs[pos + n_extras:pos + 2 * n_extras])
    pos += 2 * n_extras
    out_refs = (refs[pos:pos + n_outs[0]], refs[pos + n_outs[0]:pos + sum(n_outs)])
    w_scrs = refs[pos + sum(n_outs):]
    first_rows = (pl.program_id(1) == 0) & (pl.program_id(2) == 0)

    def run(which):
        a = a_refs[which][0]
        if transposed:
            accs = [lax.dot_general(a, scr[...], NT_DIMS, preferred_element_type=F32) for scr in w_scrs]
        else:
            accs = [jnp.dot(a, scr[...], preferred_element_type=F32) for scr in w_scrs]
        epilogues[which](accs, extras[which], out_refs[which])

    @pl.when(first_rows)
    def _new_columns():
        for w_ref, scr in zip(w_refs, w_scrs):
            scr[...] = w_ref[0].astype(BF16)
        run(1)

    run(0)


def _proj(a_p, a_s, w, layer, col_offsets, n_cols, epilogues, extras_p, extras_s, outs_p, outs_s, name,
          tn=COL_TILE, transposed=False):
    b, t, k = a_p.shape
    r = a_s.shape[1]
    tm = min(ROW_TILE, t)
    tn = min(tn, n_cols)
    assert t % tm == 0 and n_cols % tn == 0 and all(off % tn == 0 for off in col_offsets)
    assert len(extras_p) == len(extras_s)
    in_specs = [pl.BlockSpec((1, tm, k), lambda n, i, m: (i, m, 0)),
                pl.BlockSpec((1, r, k), lambda n, i, m: (0, 0, 0))]
    for off in col_offsets:
        if transposed:
            in_specs.append(pl.BlockSpec((1, tn, k), lambda n, i, m, ob=off // tn: (layer, ob + n, 0)))
        else:
            in_specs.append(pl.BlockSpec((1, k, tn), lambda n, i, m, ob=off // tn: (layer, 0, ob + n)))
    in_specs += [spec for _, spec in extras_p] + [spec for _, spec in extras_s]
    outs = list(outs_p) + list(outs_s)
    results = pl.pallas_call(
        functools.partial(_proj_kernel, n_w=len(col_offsets), n_extras=len(extras_p),
                          n_outs=(len(outs_p), len(outs_s)), epilogues=epilogues, transposed=transposed),
        grid=(n_cols // tn, b, t // tm),
        in_specs=in_specs,
        out_specs=[spec for _, spec in outs],
        out_shape=[sds for sds, _ in outs],
        scratch_shapes=[pltpu.VMEM((tn, k) if transposed else (k, tn), BF16) for _ in col_offsets],
        compiler_params=_params("arbitrary", "arbitrary", "arbitrary"),
        name=name,
    )(a_p, a_s, *([w] * len(col_offsets)), *[arr for arr, _ in extras_p], *[arr for arr, _ in extras_s])
    return results[:len(outs_p)], results[len(outs_p):]


def _plain_outs(b, t, r, n_cols, dtype, tn=COL_TILE):
    return ([(jax.ShapeDtypeStruct((b, t, n_cols), dtype), _prompt_tile_spec(t, n_cols, tn))],
            [(jax.ShapeDtypeStruct((1, r, n_cols), dtype), _sample_tile_spec(r, n_cols, tn))])


def _prompt_tile_spec(t, n_cols, tn=COL_TILE):
    return pl.BlockSpec((1, min(ROW_TILE, t), min(tn, n_cols)), lambda n, i, m: (i, m, n))


def _prompt_vec_spec(n_cols, tn=COL_TILE):
    return pl.BlockSpec((1, 1, min(tn, n_cols)), lambda n, i, m: (i, 0, n))


def _sample_tile_spec(r, n_cols, tn=COL_TILE):
    return pl.BlockSpec((1, r, min(tn, n_cols)), lambda n, i, m: (0, 0, n))


def _const_spec(shape):
    return pl.BlockSpec(shape, lambda n, i, m: (0,) * len(shape))


def _epi_silu(accs, extra, outs):
    outs[0][0] = _silu(accs[0]).astype(outs[0].dtype)


def _epi_glu(accs, extra, outs):
    outs[0][0] = (accs[0] * jax.nn.sigmoid(accs[1])).astype(outs[0].dtype)


def _epi_residual(accs, extra, outs):
    x_ref, gate_ref = extra
    outs[0][0] = x_ref[0] + gate_ref[0] * accs[0]


def _epi_log_sigmoid(accs, extra, outs, *, n_valid):
    x = accs[0] + extra[0][...]
    y = jnp.minimum(x, 0.0) - jnp.log1p(jnp.exp(-jnp.abs(x)))
    outs[0][0] = jnp.where(_iota2(y.shape, 1) < n_valid, y, 0.0)


def _epi_heads(accs, extra, outs, *, dh, scale=None):
    acc = accs[0]
    rows, heads = acc.shape[0], acc.shape[1] // dh
    by_head = outs[1].reshape(rows * heads, dh) if len(outs) > 1 else None
    for j in range(heads):
        blk = acc[:, j * dh:(j + 1) * dh]
        if scale is not None:
            ms = jnp.mean(blk * blk, axis=-1, keepdims=True)
            blk = blk * lax.rsqrt(ms + RMS_EPS) * (extra[0][...] * scale)
        outs[0][0, :, j * dh:(j + 1) * dh] = blk.astype(outs[0].dtype)
        if by_head is not None:
            by_head[pl.ds(j, rows, stride=heads), :] = blk


def _cumsum_kernel(x_ref, o_ref):
    t = x_ref.shape[1]
    ch = min(CUMSUM_CHUNK, t)
    tri = (_iota2((ch, ch), 1) <= _iota2((ch, ch), 0)).astype(F32)
    carry = jnp.zeros((1, x_ref.shape[2]), F32)
    for i in range(t // ch):
        blk = x_ref[0, i * ch:(i + 1) * ch, :]
        cs = jnp.dot(tri, blk, precision=HIGHEST, preferred_element_type=F32) + carry
        o_ref[0, i * ch:(i + 1) * ch, :] = cs
        carry = cs[ch - 1:ch, :]


def _cumsum_t(x):
    b, t, n = x.shape
    return pl.pallas_call(
        _cumsum_kernel,
        grid=(b,),
        in_specs=[pl.BlockSpec((1, t, n), lambda i: (i, 0, 0))],
        out_specs=pl.BlockSpec((1, t, n), lambda i: (i, 0, 0)),
        out_shape=jax.ShapeDtypeStruct((b, t, n), F32),
        compiler_params=_params("arbitrary"),
        name="cumsum_t",
    )(x)


def _attn_kernel(q_ref, k_ref, v_ref, cq_ref, ck_ref, sg_ref, o_ref, vb_scr, m_scr, acc_scr,
                 *, tile, n_q, n_groups):
    h = pl.program_id(1)
    i = pl.program_id(2)
    dh = q_ref.shape[2]

    @pl.when(i == 0)
    def _augment_v():
        vb_scr[:, :dh] = v_ref[0]
        vb_scr[:, dh:] = jnp.where(_iota2((vb_scr.shape[0], dh), 1) == 0, 1.0, 0.0).astype(BF16)

    rows = [slice(a * tile, (a + 1) * tile) for a in range(n_q)]
    qs = [q_ref[0, r, :] for r in rows]
    lane = _iota2((tile, cq_ref.shape[2]), 1)
    cqs = [jnp.sum(jnp.where(lane == h, cq_ref[0, r, :], 0.0), axis=-1, keepdims=True) * LOG2E for r in rows]

    m_scr[...] = jnp.full(m_scr.shape, NEG_INF, F32)
    acc_scr[...] = jnp.zeros(acc_scr.shape, F32)

    def step(a, j, masked):
        start = j * tile
        s = lax.dot_general(qs[a], k_ref[0, pl.ds(start, tile), :], NT_DIMS, preferred_element_type=F32)
        s = s + cqs[a] - ck_ref[0, 0, j] * LOG2E
        if masked:
            s = jnp.where(_iota2(s.shape, 1) <= _iota2(s.shape, 0), s, NEG_INF)
        m_old = m_scr[a]
        m_new = jnp.maximum(m_old, jnp.max(s, axis=-1, keepdims=True))
        p = jnp.exp2(s - jnp.concatenate([m_new] * (tile // LANES), axis=1)).astype(BF16)
        pv = jnp.dot(p, vb_scr[pl.ds(start, tile), :], preferred_element_type=F32)
        alpha = jnp.exp2(m_old - m_new)
        acc_scr[a] = jnp.concatenate([alpha] * (2 * dh // LANES), axis=1) * acc_scr[a] + pv
        m_scr[a] = m_new

    def run_group(group):
        first = group * n_q
        for j in range(first):
            for a in range(n_q):
                step(a, j, False)
        for jj in range(n_q):
            for a in range(jj, n_q):
                step(a, first + jj, a == jj)
        for a, r in enumerate(rows):
            acc = acc_scr[a]
            o_ref[0, r, :] = (acc[:, :dh] / acc[:, dh:dh + 1] * sg_ref[0, r, :]).astype(o_ref.dtype)

    for group in range(n_groups):
        pl.when(i == group)(functools.partial(run_group, group))


def _prompt_attention(q, k, v, cum, sg, n_heads):
    b, t, da = q.shape
    dh = da // n_heads
    tile = min(ATTN_TILE, t)
    n_q = min(ATTN_HALVES, t // tile)
    group = n_q * tile
    assert t % group == 0 and dh % LANES == 0
    cum_t = jnp.transpose(cum[:, :, :n_heads], (0, 2, 1)).reshape(b, n_heads, t // tile, 1, tile)
    return pl.pallas_call(
        functools.partial(_attn_kernel, tile=tile, n_q=n_q, n_groups=t // group),
        grid=(b, n_heads, t // group),
        in_specs=[
            pl.BlockSpec((1, group, dh), lambda bi, h, i: (bi, i, h)),
            pl.BlockSpec((1, t, dh), lambda bi, h, i: (bi, 0, h)),
            pl.BlockSpec((1, t, dh), lambda bi, h, i: (bi, 0, h)),
            pl.BlockSpec((1, group, cum.shape[2]), lambda bi, h, i: (bi, i, 0)),
            pl.BlockSpec((1, 1, t // tile, 1, tile), lambda bi, h, i: (bi, h, 0, 0, 0)),
            pl.BlockSpec((1, group, dh), lambda bi, h, i: (bi, i, h)),
        ],
        out_specs=pl.BlockSpec((1, group, dh), lambda bi, h, i: (bi, i, h)),
        out_shape=jax.ShapeDtypeStruct((b, t, da), BF16),
        scratch_shapes=[pltpu.VMEM((t, 2 * dh), BF16),
                        pltpu.VMEM((n_q, tile, LANES), F32), pltpu.VMEM((n_q, tile, 2 * dh), F32)],
        compiler_params=_params("arbitrary", "arbitrary", "arbitrary"),
        name="prompt_attention",
    )(q, k, v, cum, cum_t, sg)


def _decode_bias_kernel(pt_ref, *refs, n_heads):
    lf_refs, o_ref, carry_scr = refs[:-2], refs[-2], refs[-1]

    @pl.when(pl.program_id(1) == 0)
    def _init():
        carry_scr[...] = jnp.zeros(carry_scr.shape, F32)

    lf = jnp.concatenate([r[0] for r in lf_refs], axis=0)
    n = lf.shape[0]
    l0, l1 = _iota2((LANES, LANES), 0), _iota2((LANES, LANES), 1)
    same_head = l0 % n_heads == l1 % n_heads
    later_in_row = (same_head & (l0 // n_heads > l1 // n_heads)).astype(F32)
    row_total = jnp.dot(lf, same_head.astype(F32), precision=HIGHEST, preferred_element_type=F32)
    later_rows = (_iota2((n, n), 1) > _iota2((n, n), 0)).astype(F32)
    suffix = (jnp.dot(lf, later_in_row, precision=HIGHEST, preferred_element_type=F32)
              + jnp.dot(later_rows, row_total, precision=HIGHEST, preferred_element_type=F32)
              + carry_scr[0:1, :])
    carry_scr[...] = carry_scr[...] + jnp.dot(jnp.ones((SUBLANES, n), F32), row_total,
                                              precision=HIGHEST, preferred_element_type=F32)
    o_ref[0] = (suffix * LOG2E).reshape(o_ref.shape[1:])


def _decode_bias(page_table, lf_cache, n_heads):
    db, n_pages = page_table.shape
    rows = lf_cache.shape[1]
    pb = min(BIAS_PAGES, n_pages)
    groups = n_pages // pb
    assert n_pages % pb == 0

    def page_spec(k):
        return pl.BlockSpec((1, rows, LANES), lambda bi, g, pt: (pt[bi, (groups - 1 - g) * pb + k], 0, 0))

    grid_spec = pltpu.PrefetchScalarGridSpec(
        num_scalar_prefetch=1,
        grid=(db, groups),
        in_specs=[page_spec(k) for k in range(pb)],
        out_specs=pl.BlockSpec((1, pb, rows, LANES), lambda bi, g, pt: (bi, groups - 1 - g, 0, 0)),
        scratch_shapes=[pltpu.VMEM((SUBLANES, LANES), F32)],
    )
    return pl.pallas_call(
        functools.partial(_decode_bias_kernel, n_heads=n_heads),
        grid_spec=grid_spec,
        out_shape=jax.ShapeDtypeStruct((db, n_pages, rows, LANES), F32),
        compiler_params=_params("arbitrary", "arbitrary"),
        name="decode_bias",
    )(page_table, *([lf_cache] * pb))


def _decode_kernel(pt_ref, wt_ref, *refs, n_heads, n_new, n_par):
    page_refs = [refs[3 * i:3 * i + 3] for i in range(n_par)]
    kn_ref, vn_ref, lfc_ref, lfr_ref, sg_ref, o_ref, m_scr, l_scr, acc_scr, colb_scr, p_scr = refs[3 * n_par:]
    p = pl.program_id(1)
    n_steps = pl.num_programs(1)
    hs = wt_ref.shape[1]
    n_rows = page_refs[0][2].shape[2]

    @pl.when(p == 0)
    def _init():
        m_scr[...] = jnp.full(m_scr.shape, NEG_INF, F32)
        l_scr[...] = jnp.zeros(l_scr.shape, F32)
        acc_scr[...] = jnp.zeros(acc_scr.shape, F32)
        i0, i1 = _iota2((hs, hs), 0), _iota2((hs, hs), 1)
        tri = ((i0 // n_new == i1 // n_new) & (i1 % n_new <= i0 % n_new)).astype(F32)
        cum_col = jnp.dot(tri, lfc_ref[0], precision=HIGHEST, preferred_element_type=F32) * LOG2E
        other_head = _iota2((hs, LANES), 0) // n_new != _iota2((hs, LANES), 1) % n_heads
        colb_scr[0] = cum_col
        colb_scr[1] = jnp.where(other_head, NEG_INF, cum_col)

    def softmax_update(st, s_chunks, v_bf16):
        m_old = m_scr[st]
        mx = s_chunks[0]
        for sc in s_chunks[1:]:
            mx = jnp.maximum(mx, sc)
        m_new = jnp.maximum(m_old, jnp.max(mx, axis=-1, keepdims=True))
        alpha = jnp.exp2(m_old - m_new)
        psum = None
        for c, sc in enumerate(s_chunks):
            pc = jnp.exp2(sc - m_new[:, :sc.shape[1]])
            p_scr[st, :, c * LANES:c * LANES + sc.shape[1]] = pc.astype(BF16)
            psum = pc if psum is None else psum + pc
        width = (len(s_chunks) - 1) * LANES + s_chunks[-1].shape[1]
        m_scr[st] = m_new
        l_scr[st] = alpha * l_scr[st] + jnp.sum(psum, axis=-1, keepdims=True)
        pv = jnp.dot(p_scr[st, :, :width], v_bf16, preferred_element_type=F32)
        acc_scr[st] = alpha * acc_scr[st] + pv

    row_bias = colb_scr[1]
    for st, (k_ref, v_ref, bias_ref) in enumerate(page_refs):
        kb = k_ref[0].astype(BF16)
        r = lax.dot_general(wt_ref[0], kb, NT_DIMS, preferred_element_type=F32)
        bias = bias_ref[0, 0]
        chunks = [r[:, c * LANES:(c + 1) * LANES] + bias[c:c + 1, :] + row_bias for c in range(n_rows)]
        softmax_update(st, chunks, v_ref[0].astype(BF16))

    @pl.when(p == n_steps - 1)
    def _finish():
        sh = kn_ref.shape[1]
        c0, c1 = _iota2((sh, sh), 0), _iota2((sh, sh), 1)
        tri = ((c0 % n_heads == c1 % n_heads) & (c0 // n_heads <= c1 // n_heads)).astype(F32)
        cum_row = jnp.dot(lfr_ref[0], tri, precision=HIGHEST, preferred_element_type=F32) * LOG2E
        rn = lax.dot_general(wt_ref[0], kn_ref[0].astype(BF16), NT_DIMS, preferred_element_type=F32)
        row = _iota2((hs, sh), 0)
        col = _iota2((hs, sh), 1)
        ok = (row // n_new == col % n_heads) & (col // n_heads <= row % n_new)
        sn = jnp.where(ok, rn + colb_scr[0][:, :sh] - cum_row[0:1, :], NEG_INF)
        softmax_update(0, [sn], vn_ref[0].astype(BF16))
        m = m_scr[0]
        for st in range(1, n_par):
            m = jnp.maximum(m, m_scr[st])
        l = jnp.zeros(m.shape, F32)
        acc = jnp.zeros(acc_scr.shape[1:], F32)
        for st in range(n_par):
            w = jnp.exp2(m_scr[st] - m)
            l = l + w * l_scr[st]
            acc = acc + w * acc_scr[st]
        o_ref[0] = (acc / l * sg_ref[0]).astype(o_ref.dtype)


def _decode_attention(page_table, wt, k_cache, v_cache, bias, k_new, v_new, lf_col, lf_row, sg,
                      n_heads, n_new):
    db, n_pages = page_table.shape
    hs, dh = wt.shape[1:]
    n_phys, page_rows, _ = k_cache.shape
    bias_rows = bias.shape[2]
    sh = k_new.shape[1]
    n_par = DECODE_PAGES
    assert dh == LANES and hs <= LANES and LANES % n_heads == 0 and n_pages % n_par == 0

    def seq(bi, p, pt):
        return (bi, 0, 0)

    page_specs = []
    for i in range(n_par):
        def page(bi, p, pt, i=i):
            return (pt[bi, p * n_par + i], 0, 0)
        page_specs += [pl.BlockSpec((1, page_rows, dh), page), pl.BlockSpec((1, page_rows, dh), page),
                       pl.BlockSpec((1, 1, bias_rows, LANES), lambda bi, p, pt, i=i: (bi, p * n_par + i, 0, 0))]

    grid_spec = pltpu.PrefetchScalarGridSpec(
        num_scalar_prefetch=1,
        grid=(db, n_pages // n_par),
        in_specs=[pl.BlockSpec((1, hs, dh), seq)] + page_specs + [
            pl.BlockSpec((1, sh, dh), seq),
            pl.BlockSpec((1, sh, dh), seq),
            pl.BlockSpec((1, hs, LANES), seq),
            pl.BlockSpec((1, SUBLANES, sh), seq),
            pl.BlockSpec((1, hs, dh), seq),
        ],
        out_specs=pl.BlockSpec((1, hs, dh), seq),
        scratch_shapes=[
            pltpu.VMEM((n_par, hs, LANES), F32),
            pltpu.VMEM((n_par, hs, LANES), F32),
            pltpu.VMEM((n_par, hs, dh), F32),
            pltpu.VMEM((2, hs, LANES), F32),
            pltpu.VMEM((n_par, hs, page_rows), BF16),
        ],
    )
    return pl.pallas_call(
        functools.partial(_decode_kernel, n_heads=n_heads, n_new=n_new, n_par=n_par),
        grid_spec=grid_spec,
        out_shape=jax.ShapeDtypeStruct((db, hs, dh), BF16),
        compiler_params=_params("arbitrary", "arbitrary"),
        name="decode_attention",
    )(page_table, wt, *([k_cache, v_cache, bias] * n_par), k_new, v_new, lf_col, lf_row, sg)


def _conv_kernel(u_ref, hist_ref, w_ref, b_ref, o_ref, win_ref, shift_ref, *, width, chunk):
    tt = u_ref.shape[1]
    n = HIST_ROWS + tt
    pad = HIST_ROWS - (width - 1)

    @pl.when(pl.program_id(2) == 0)
    def _load_history():
        win_ref[0:HIST_ROWS, :] = hist_ref[0]

    win_ref[HIST_ROWS:n, :] = u_ref[0]
    for r in range(1, SUBLANES):
        shift_ref[r, 0:n - SUBLANES, :] = win_ref[r:r + n - SUBLANES, :]
    bias = b_ref[...]
    for rc in range(tt // chunk):
        base = rc * chunk
        acc = jnp.broadcast_to(bias, (chunk, bias.shape[1]))
        for k in range(width):
            a, r = divmod(pad + k, SUBLANES)
            lo = base + a * SUBLANES
            rows = win_ref[lo:lo + chunk, :] if r == 0 else shift_ref[r, lo:lo + chunk, :]
            acc = acc + rows * w_ref[k:k + 1, :]
        o_ref[0, base:base + chunk, :] = acc
    win_ref[0:HIST_ROWS, :] = win_ref[tt:n, :]


def _depthwise_conv(u, hist, w, bias):
    b, t, c = u.shape
    width = w.shape[0]
    tt = min(CONV_ROWS, t)
    tc = min(CONV_COLS, c)
    chunk = min(CONV_CHUNK, tt)
    assert t % tt == 0 and c % tc == 0 and tt % chunk == 0 and width - 1 <= HIST_ROWS
    w_pad = jnp.pad(w, ((0, HIST_ROWS - width), (0, 0)))
    return pl.pallas_call(
        functools.partial(_conv_kernel, width=width, chunk=chunk),
        grid=(b, c // tc, t // tt),
        in_specs=[
            pl.BlockSpec((1, tt, tc), lambda bi, ci, ti: (bi, ti, ci)),
            pl.BlockSpec((1, HIST_ROWS, tc), lambda bi, ci, ti: (bi, 0, ci)),
            pl.BlockSpec((HIST_ROWS, tc), lambda bi, ci, ti: (0, ci)),
            pl.BlockSpec((1, tc), lambda bi, ci, ti: (0, ci)),
        ],
        out_specs=pl.BlockSpec((1, tt, tc), lambda bi, ci, ti: (bi, ti, ci)),
        out_shape=jax.ShapeDtypeStruct((b, t, c), F32),
        scratch_shapes=[pltpu.VMEM((HIST_ROWS + tt, tc), F32),
                        pltpu.VMEM((SUBLANES, HIST_ROWS + tt, tc), F32)],
        compiler_params=_params("arbitrary", "arbitrary", "arbitrary"),
        name="depthwise_conv",
    )(u, hist, w_pad, bias.reshape(1, c))


def _ln_gate_kernel(x_ref, sg_ref, g_ref, b_ref, o_ref):
    x = x_ref[0]
    mu = jnp.mean(x, axis=-1, keepdims=True)
    xc = x - mu
    var = jnp.mean(xc * xc, axis=-1, keepdims=True)
    z = xc * lax.rsqrt(var + LN_EPS) * g_ref[...] + b_ref[...]
    o_ref[0] = (_silu(z) * sg_ref[0]).astype(o_ref.dtype)


def _ln_gate(x, sg, g, bias):
    b, t, c = x.shape
    tr = min(NORM_ROWS, t)
    assert t % tr == 0
    row = pl.BlockSpec((1, tr, c), lambda i, j: (i, j, 0))
    vec = pl.BlockSpec((1, c), lambda i, j: (0, 0))
    return pl.pallas_call(
        _ln_gate_kernel,
        grid=(b, t // tr),
        in_specs=[row, row, vec, vec],
        out_specs=row,
        out_shape=jax.ShapeDtypeStruct((b, t, c), BF16),
        compiler_params=_params("arbitrary", "arbitrary"),
        name="ln_gate",
    )(x, sg, g.reshape(1, c), bias.reshape(1, c))


def kernel(x_prompt, x_sample, c_prompt, c_sample, cache_k, cache_v, cache_logf, state_conv,
           page_table, w_ada, b_ada, norm_g, w_in_attn, b_f, q_gain, k_gain, w_out_attn,
           w_in_conv, dw_w, dw_b, ln_g, ln_b, w_out_conv):
    b, t, d = x_prompt.shape
    db, s, _ = x_sample.shape
    rs = db * s
    depth = w_ada.shape[0]
    n_heads, dh = b_f.shape[1], q_gain.shape[1]
    da = n_heads * dh
    dc = dw_w.shape[2]
    conv_state = dw_w.shape[1] - 1
    page = cache_k.shape[2]
    n_phys = cache_k.shape[1]

    n_c = b + db
    c_all = jnp.pad(jnp.concatenate([c_prompt, c_sample], axis=0), ((0, -n_c % SUBLANES), (0, 0)))
    mod = _ada_mod(c_all, w_ada, b_ada)

    xp = x_prompt
    xs = x_sample.reshape(1, rs, d)
    outs = {name: [] for name in ("kp", "vp", "fp", "ks", "vs", "fs", "cp", "cs")}

    def residual_out(a_p, a_s, w, j, xp, xs, gate_p, gate_s, name):
        tn = COL_TILE // 2
        (xp,), (xs,) = _proj(a_p, a_s, w, j, [0], d, (_epi_residual, _epi_residual),
                             [(xp, _prompt_tile_spec(t, d, tn)), (gate_p, _prompt_vec_spec(d, tn))],
                             [(xs, _sample_tile_spec(rs, d, tn)), (gate_s, _sample_tile_spec(rs, d, tn))],
                             *_plain_outs(b, t, rs, d, F32, tn=tn), name, tn=tn)
        return xp, xs

    for i in range(depth):
        j = i // 2
        mod_p = mod[i, :b][:, None, :]
        mod_s = jnp.repeat(mod[i, b:n_c], s, axis=0)[None]
        hp = _norm_mod(xp, norm_g[i], mod_p[..., d:2 * d], mod_p[..., :d])
        hs = _norm_mod(xs, norm_g[i], mod_s[..., d:2 * d], mod_s[..., :d])
        gate_p, gate_s = mod_p[..., 2 * d:], mod_s[..., 2 * d:]

        if i % 2 == 0:
            w_in_t = jnp.transpose(w_in_attn, (0, 2, 1))

            def in_proj(col, n_cols, epilogues, extra, outs_p, outs_s, name, tn=COL_TILE):
                return _proj(hp, hs, w_in_t, j, [col], n_cols, epilogues, extra, extra, outs_p, outs_s, name,
                             tn=tn, transposed=True)

            hb = min(COL_TILE, da) // dh
            kv_outs_p = [(jax.ShapeDtypeStruct((b, t, da), BF16), _prompt_tile_spec(t, da)),
                         (jax.ShapeDtypeStruct((b, t, n_heads // hb, hb, dh), F32),
                          pl.BlockSpec((1, min(ROW_TILE, t), 1, hb, dh), lambda n, i, m: (i, m, n, 0, 0)))]
            kv_outs_s = _plain_outs(b, t, rs, da, F32)[1]
            q_epi = functools.partial(_epi_heads, dh=dh, scale=dh ** -0.5 * LOG2E)
            k_epi = functools.partial(_epi_heads, dh=dh, scale=1.0)
            v_epi = functools.partial(_epi_heads, dh=dh)
            (q_p,), (q_s,) = in_proj(0, da, (q_epi, q_epi), [(q_gain[j][None], _const_spec((1, dh)))],
                                     *_plain_outs(b, t, rs, da, BF16), "proj_q")
            (kb_p, k_p), (k_s,) = in_proj(da, da, (k_epi, k_epi), [(k_gain[j][None], _const_spec((1, dh)))],
                                          kv_outs_p, kv_outs_s, "proj_k")
            (vb_p, v_p), (v_s,) = in_proj(2 * da, da, (v_epi, v_epi), [], kv_outs_p, kv_outs_s, "proj_v")
            (sg_p,), (sg_s,) = in_proj(3 * da, da, (_epi_silu, _epi_silu), [],
                                       *_plain_outs(b, t, rs, da, BF16), "proj_gate")
            b_f_row = jnp.pad(b_f[j][None], ((0, 0), (0, LANES - n_heads)))
            lf_epi = functools.partial(_epi_log_sigmoid, n_valid=n_heads)
            (lf_p,), (lf_s,) = in_proj(4 * da, LANES, (lf_epi, lf_epi), [(b_f_row, _const_spec((1, LANES)))],
                                       *_plain_outs(b, t, rs, LANES, F32, tn=LANES), "proj_logf", tn=LANES)

            a_p = _prompt_attention(q_p, kb_p, vb_p, _cumsum_t(lf_p), sg_p, n_heads)
            outs["kp"].append(k_p.reshape(b, t, n_heads, dh))
            outs["vp"].append(v_p.reshape(b, t, n_heads, dh))
            outs["fp"].append(lf_p[:, :, :n_heads])

            lf_new = lf_s[0, :, :n_heads].reshape(db, s, n_heads)

            def rows_head_query(x):
                x = x.reshape(db, s, n_heads, dh)
                return jnp.transpose(x, (0, 2, 1, 3)).reshape(db, n_heads * s, dh)

            lf_col = jnp.transpose(lf_new, (0, 2, 1)).reshape(db, n_heads * s, 1)
            lf_col = jnp.broadcast_to(lf_col, (db, n_heads * s, LANES))
            lf_row = jnp.broadcast_to(lf_new.reshape(db, 1, s * n_heads), (db, SUBLANES, s * n_heads))
            bias = _decode_bias(page_table, cache_logf[j].reshape(n_phys, page * n_heads // LANES, LANES), n_heads)
            a_s = _decode_attention(
                page_table, rows_head_query(q_s),
                cache_k[j].reshape(n_phys, page * n_heads, dh),
                cache_v[j].reshape(n_phys, page * n_heads, dh),
                bias, k_s.reshape(db, s * n_heads, dh), v_s.reshape(db, s * n_heads, dh),
                lf_col, lf_row, rows_head_query(sg_s), n_heads, s)
            a_s = jnp.transpose(a_s.reshape(db, n_heads, s, dh), (0, 2, 1, 3)).reshape(1, rs, da)
            outs["ks"].append(k_s.reshape(db, s, n_heads, dh))
            outs["vs"].append(v_s.reshape(db, s, n_heads, dh))
            outs["fs"].append(lf_new)

            xp, xs = residual_out(a_p, a_s, w_out_attn, j, xp, xs, gate_p, gate_s, "out_attn")
        else:
            (u_p,), (u_s,) = _proj(hp, hs, w_in_conv, j, [0, dc], dc, (_epi_glu, _epi_glu), [], [],
                                   *_plain_outs(b, t, rs, dc, F32, tn=COL_TILE // 2), "proj_glu", tn=COL_TILE // 2)
            (sg_p,), (sg_s,) = _proj(hp, hs, w_in_conv, j, [2 * dc], dc, (_epi_silu, _epi_silu), [], [],
                                     *_plain_outs(b, t, rs, dc, BF16), "proj_gate")

            def conv_branch(u, sg, hist):
                rows = u.shape[1]
                pad_rows = -rows % SUBLANES
                u = jnp.pad(u, ((0, 0), (0, pad_rows), (0, 0)))
                sg = jnp.pad(sg, ((0, 0), (0, pad_rows), (0, 0)))
                hist = jnp.pad(hist, ((0, 0), (HIST_ROWS - conv_state, 0), (0, 0)))
                conv = _depthwise_conv(u, hist, dw_w[j], dw_b[j])
                return _ln_gate(conv, sg, ln_g[j], ln_b[j])[:, :rows]

            a_p = conv_branch(u_p, sg_p, jnp.zeros((b, conv_state, dc), F32))
            outs["cp"].append(u_p[:, t - conv_state:, :])
            u_s = u_s.reshape(db, s, dc)
            a_s = conv_branch(u_s, sg_s.reshape(db, s, dc), state_conv[j]).reshape(1, rs, dc)
            outs["cs"].append(jnp.concatenate([state_conv[j], u_s], axis=1)[:, -conv_state:, :])

            xp, xs = residual_out(a_p, a_s, w_out_conv, j, xp, xs, gate_p, gate_s, "out_conv")

    return (xp, xs.reshape(db, s, d), jnp.stack(outs["kp"]), jnp.stack(outs["vp"]), jnp.stack(outs["fp"]),
            jnp.stack(outs["ks"]), jnp.stack(outs["vs"]), jnp.stack(outs["fs"]),
            jnp.stack(outs["cp"]), jnp.stack(outs["cs"]))
```

```python
import functools
import math

import jax
import jax.numpy as jnp
from jax import lax
from jax.experimental import pallas as pl
from jax.experimental.pallas import tpu as pltpu

F32 = jnp.float32
BF16 = jnp.bfloat16
HIGHEST = lax.Precision.HIGHEST

RMS_EPS = 1e-6
LN_EPS = 1e-5
NEG_INF = -1e30
LOG2E = math.log2(math.e)

LANES = 128
SUBLANES = 8
VMEM_LIMIT_BYTES = 63 * 1024 * 1024

ROW_TILE = 512
COL_TILE = 1024
ADA_COLS = 512
ATTN_TILE = 512
NORM_ROWS = 256
CONV_ROWS = 256
CONV_COLS = 512
CONV_CHUNK = 32
HIST_ROWS = 32
CUMSUM_CHUNK = 256
BIAS_PAGES = 16
DECODE_PAGES = 4
DECODE_SLOTS = 3
ATTN_HALVES = 4

NT_DIMS = (((1,), (1,)), ((), ()))


def _params(*sem):
    return pltpu.CompilerParams(dimension_semantics=sem, vmem_limit_bytes=VMEM_LIMIT_BYTES)


def _silu(x):
    return x * jax.nn.sigmoid(x)


def _iota2(shape, dim):
    return lax.broadcasted_iota(jnp.int32, shape, dim)


def _ada_kernel(c_ref, w_ref, b_ref, o_ref):
    sc = _silu(c_ref[...]).astype(BF16)
    w = w_ref[0].astype(BF16)
    o_ref[0] = jnp.dot(sc, w, preferred_element_type=F32) + b_ref[0]


def _ada_mod(c, w_ada, b_ada):
    depth, d, n3 = w_ada.shape
    r = c.shape[0]
    tn = min(ADA_COLS, n3)
    assert n3 % tn == 0 and r % SUBLANES == 0
    return pl.pallas_call(
        _ada_kernel,
        grid=(depth, n3 // tn),
        in_specs=[
            pl.BlockSpec((r, d), lambda i, n: (0, 0)),
            pl.BlockSpec((1, d, tn), lambda i, n: (i, 0, n)),
            pl.BlockSpec((1, 1, tn), lambda i, n: (i, 0, n)),
        ],
        out_specs=pl.BlockSpec((1, r, tn), lambda i, n: (i, 0, n)),
        out_shape=jax.ShapeDtypeStruct((depth, r, n3), F32),
        compiler_params=_params("arbitrary", "arbitrary"),
        name="ada_mod",
    )(c, w_ada, b_ada.reshape(depth, 1, n3))


def _norm_mod_kernel(x_ref, g_ref, sc_ref, sh_ref, o_ref):
    x = x_ref[0]
    ms = jnp.mean(x * x, axis=-1, keepdims=True)
    y = x * lax.rsqrt(ms + RMS_EPS) * g_ref[...]
    o_ref[0] = (y * (1.0 + sc_ref[0]) + sh_ref[0]).astype(o_ref.dtype)


def _norm_mod(x, g, scale, shift):
    b, t, d = x.shape
    tr = min(NORM_ROWS, t)
    assert t % tr == 0
    per_row = scale.shape[1] != 1
    mod_spec = (pl.BlockSpec((1, tr, d), lambda i, j: (i, j, 0)) if per_row
                else pl.BlockSpec((1, 1, d), lambda i, j: (i, 0, 0)))
    return pl.pallas_call(
        _norm_mod_kernel,
        grid=(b, t // tr),
        in_specs=[
            pl.BlockSpec((1, tr, d), lambda i, j: (i, j, 0)),
            pl.BlockSpec((1, d), lambda i, j: (0, 0)),
            mod_spec, mod_spec,
        ],
        out_specs=pl.BlockSpec((1, tr, d), lambda i, j: (i, j, 0)),
        out_shape=jax.ShapeDtypeStruct((b, t, d), BF16),
        compiler_params=_params("arbitrary", "arbitrary"),
        name="norm_mod",
    )(x, g.reshape(1, d), scale, shift)


def _proj_kernel(*refs, n_w, n_extras, n_outs, epilogues, transposed):
    a_refs = refs[:2]
    w_refs = refs[2:2 + n_w]
    pos = 2 + n_w
    extras = (refs[pos:pos + n_extras], refs[pos + n_extras:pos + 2 * n_extras])
    pos += 2 * n_extras
    out_refs = (refs[pos:pos + n_outs[0]], refs[pos + n_outs[0]:pos + sum(n_outs)])
    w_scrs = refs[pos + sum(n_outs):]
    first_rows = (pl.program_id(1) == 0) & (pl.program_id(2) == 0)

    def run(which):
        a = a_refs[which][0]
        if transposed:
            accs = [lax.dot_general(a, scr[...], NT_DIMS, preferred_element_type=F32) for scr in w_scrs]
        else:
            accs = [jnp.dot(a, scr[...], preferred_element_type=F32) for scr in w_scrs]
        epilogues[which](accs, extras[which], out_refs[which])

    @pl.when(first_rows)
    def _new_columns():
        for w_ref, scr in zip(w_refs, w_scrs):
            scr[...] = w_ref[0].astype(BF16)
        run(1)

    run(0)


def _proj(a_p, a_s, w, layer, col_offsets, n_cols, epilogues, extras_p, extras_s, outs_p, outs_s, name,
          tn=COL_TILE, tm=ROW_TILE, transposed=False):
    b, t, k = a_p.shape
    r = a_s.shape[1]
    tm = min(tm, t)
    tn = min(tn, n_cols)
    assert t % tm == 0 and n_cols % tn == 0 and all(off % tn == 0 for off in col_offsets)
    assert len(extras_p) == len(extras_s)
    in_specs = [pl.BlockSpec((1, tm, k), lambda n, i, m: (i, m, 0)),
                pl.BlockSpec((1, r, k), lambda n, i, m: (0, 0, 0))]
    for off in col_offsets:
        if transposed:
            in_specs.append(pl.BlockSpec((1, tn, k), lambda n, i, m, ob=off // tn: (layer, ob + n, 0)))
        else:
            in_specs.append(pl.BlockSpec((1, k, tn), lambda n, i, m, ob=off // tn: (layer, 0, ob + n)))
    in_specs += [spec for _, spec in extras_p] + [spec for _, spec in extras_s]
    outs = list(outs_p) + list(outs_s)
    results = pl.pallas_call(
        functools.partial(_proj_kernel, n_w=len(col_offsets), n_extras=len(extras_p),
                          n_outs=(len(outs_p), len(outs_s)), epilogues=epilogues, transposed=transposed),
        grid=(n_cols // tn, b, t // tm),
        in_specs=in_specs,
        out_specs=[spec for _, spec in outs],
        out_shape=[sds for sds, _ in outs],
        scratch_shapes=[pltpu.VMEM((tn, k) if transposed else (k, tn), BF16) for _ in col_offsets],
        compiler_params=_params("arbitrary", "arbitrary", "arbitrary"),
        name=name,
    )(a_p, a_s, *([w] * len(col_offsets)), *[arr for arr, _ in extras_p], *[arr for arr, _ in extras_s])
    return results[:len(outs_p)], results[len(outs_p):]


def _plain_outs(b, t, r, n_cols, dtype, tn=COL_TILE, tm=ROW_TILE):
    return ([(jax.ShapeDtypeStruct((b, t, n_cols), dtype), _prompt_tile_spec(t, n_cols, tn, tm))],
            [(jax.ShapeDtypeStruct((1, r, n_cols), dtype), _sample_tile_spec(r, n_cols, tn))])


def _prompt_tile_spec(t, n_cols, tn=COL_TILE, tm=ROW_TILE):
    return pl.BlockSpec((1, min(tm, t), min(tn, n_cols)), lambda n, i, m: (i, m, n))


def _prompt_vec_spec(n_cols, tn=COL_TILE):
    return pl.BlockSpec((1, 1, min(tn, n_cols)), lambda n, i, m: (i, 0, n))


def _sample_tile_spec(r, n_cols, tn=COL_TILE):
    return pl.BlockSpec((1, r, min(tn, n_cols)), lambda n, i, m: (0, 0, n))


def _const_spec(shape):
    return pl.BlockSpec(shape, lambda n, i, m: (0,) * len(shape))


def _epi_silu(accs, extra, outs):
    outs[0][0] = _silu(accs[0]).astype(outs[0].dtype)


def _epi_glu(accs, extra, outs):
    outs[0][0] = (accs[0] * jax.nn.sigmoid(accs[1])).astype(outs[0].dtype)


def _epi_residual(accs, extra, outs):
    x_ref, gate_ref = extra
    outs[0][0] = x_ref[0] + gate_ref[0] * accs[0]


def _epi_log_sigmoid(accs, extra, outs, *, n_valid):
    x = accs[0] + extra[0][...]
    y = jnp.minimum(x, 0.0) - jnp.log1p(jnp.exp(-jnp.abs(x)))
    outs[0][0] = jnp.where(_iota2(y.shape, 1) < n_valid, y, 0.0)


def _epi_heads(accs, extra, outs, *, dh, scale=None):
    acc = accs[0]
    rows, heads = acc.shape[0], acc.shape[1] // dh
    by_head = outs[1].reshape(rows * heads, dh) if len(outs) > 1 else None
    for j in range(heads):
        blk = acc[:, j * dh:(j + 1) * dh]
        if scale is not None:
            ms = jnp.mean(blk * blk, axis=-1, keepdims=True)
            blk = blk * lax.rsqrt(ms + RMS_EPS) * (extra[0][...] * scale)
        outs[0][0, :, j * dh:(j + 1) * dh] = blk.astype(outs[0].dtype)
        if by_head is not None:
            by_head[pl.ds(j, rows, stride=heads), :] = blk


def _cumsum_kernel(x_ref, o_ref):
    t = x_ref.shape[1]
    ch = min(CUMSUM_CHUNK, t)
    tri = (_iota2((ch, ch), 1) <= _iota2((ch, ch), 0)).astype(F32)
    carry = jnp.zeros((1, x_ref.shape[2]), F32)
    for i in range(t // ch):
        blk = x_ref[0, i * ch:(i + 1) * ch, :]
        cs = jnp.dot(tri, blk, precision=HIGHEST, preferred_element_type=F32) + carry
        o_ref[0, i * ch:(i + 1) * ch, :] = cs
        carry = cs[ch - 1:ch, :]


def _cumsum_t(x):
    b, t, n = x.shape
    return pl.pallas_call(
        _cumsum_kernel,
        grid=(b,),
        in_specs=[pl.BlockSpec((1, t, n), lambda i: (i, 0, 0))],
        out_specs=pl.BlockSpec((1, t, n), lambda i: (i, 0, 0)),
        out_shape=jax.ShapeDtypeStruct((b, t, n), F32),
        compiler_params=_params("arbitrary"),
        name="cumsum_t",
    )(x)


def _attn_kernel(q_ref, k_ref, v_ref, cq_ref, ck_ref, sg_ref, o_ref, vb_scr, m_scr, acc_scr,
                 *, tile, n_q, n_groups):
    h = pl.program_id(1)
    i = pl.program_id(2)
    dh = q_ref.shape[2]

    @pl.when(i == 0)
    def _augment_v():
        vb_scr[:, :dh] = v_ref[0]
        vb_scr[:, dh:] = jnp.where(_iota2((vb_scr.shape[0], dh), 1) == 0, 1.0, 0.0).astype(BF16)

    rows = [slice(a * tile, (a + 1) * tile) for a in range(n_q)]
    qs = [q_ref[0, r, :] for r in rows]
    lane = _iota2((tile, cq_ref.shape[2]), 1)
    cqs = [jnp.sum(jnp.where(lane == h, cq_ref[0, r, :], 0.0), axis=-1, keepdims=True) * LOG2E for r in rows]

    m_scr[...] = jnp.full(m_scr.shape, NEG_INF, F32)
    acc_scr[...] = jnp.zeros(acc_scr.shape, F32)

    def step(a, j, masked):
        start = j * tile
        s = lax.dot_general(qs[a], k_ref[0, pl.ds(start, tile), :], NT_DIMS, preferred_element_type=F32)
        s = s + cqs[a] - ck_ref[0, 0, j] * LOG2E
        if masked:
            s = jnp.where(_iota2(s.shape, 1) <= _iota2(s.shape, 0), s, NEG_INF)
        m_old = m_scr[a]
        m_new = jnp.maximum(m_old, jnp.max(s, axis=-1, keepdims=True))
        p = jnp.exp2(s - jnp.concatenate([m_new] * (tile // LANES), axis=1)).astype(BF16)
        pv = jnp.dot(p, vb_scr[pl.ds(start, tile), :], preferred_element_type=F32)
        alpha = jnp.exp2(m_old - m_new)
        acc_scr[a] = jnp.concatenate([alpha] * (2 * dh // LANES), axis=1) * acc_scr[a] + pv
        m_scr[a] = m_new

    def run_group(group):
        first = group * n_q
        for j in range(first):
            for a in range(n_q):
                step(a, j, False)
        for jj in range(n_q):
            for a in range(jj, n_q):
                step(a, first + jj, a == jj)
        for a, r in enumerate(rows):
            acc = acc_scr[a]
            o_ref[0, r, :] = (acc[:, :dh] / acc[:, dh:dh + 1] * sg_ref[0, r, :]).astype(o_ref.dtype)

    for group in range(n_groups):
        pl.when(i == group)(functools.partial(run_group, group))


def _prompt_attention(q, k, v, cum, sg, n_heads):
    b, t, da = q.shape
    dh = da // n_heads
    tile = min(ATTN_TILE, t)
    n_q = min(ATTN_HALVES, t // tile)
    group = n_q * tile
    assert t % group == 0 and dh % LANES == 0
    cum_t = jnp.transpose(cum[:, :, :n_heads], (0, 2, 1)).reshape(b, n_heads, t // tile, 1, tile)
    return pl.pallas_call(
        functools.partial(_attn_kernel, tile=tile, n_q=n_q, n_groups=t // group),
        grid=(b, n_heads, t // group),
        in_specs=[
            pl.BlockSpec((1, group, dh), lambda bi, h, i: (bi, i, h)),
            pl.BlockSpec((1, t, dh), lambda bi, h, i: (bi, 0, h)),
            pl.BlockSpec((1, t, dh), lambda bi, h, i: (bi, 0, h)),
            pl.BlockSpec((1, group, cum.shape[2]), lambda bi, h, i: (bi, i, 0)),
            pl.BlockSpec((1, 1, t // tile, 1, tile), lambda bi, h, i: (bi, h, 0, 0, 0)),
            pl.BlockSpec((1, group, dh), lambda bi, h, i: (bi, i, h)),
        ],
        out_specs=pl.BlockSpec((1, group, dh), lambda bi, h, i: (bi, i, h)),
        out_shape=jax.ShapeDtypeStruct((b, t, da), BF16),
        scratch_shapes=[pltpu.VMEM((t, 2 * dh), BF16),
                        pltpu.VMEM((n_q, tile, LANES), F32), pltpu.VMEM((n_q, tile, 2 * dh), F32)],
        compiler_params=_params("arbitrary", "arbitrary", "arbitrary"),
        name="prompt_attention",
    )(q, k, v, cum, cum_t, sg)


def _select_sum(x, mask, mask_on_left):
    hi = x.astype(BF16)
    rest = x - hi.astype(F32)
    mid = rest.astype(BF16)
    lo = (rest - mid.astype(F32)).astype(BF16)
    m = jnp.where(mask, 1.0, 0.0).astype(BF16)
    total = None
    for part in (hi, mid, lo):
        prod = (jnp.dot(m, part, preferred_element_type=F32) if mask_on_left
                else jnp.dot(part, m, preferred_element_type=F32))
        total = prod if total is None else total + prod
    return total


def _decode_bias_kernel(pt_ref, *refs, n_heads):
    lf_refs, o_ref, carry_scr = refs[:-2], refs[-2], refs[-1]

    @pl.when(pl.program_id(1) == 0)
    def _init():
        carry_scr[...] = jnp.zeros(carry_scr.shape, F32)

    lf = jnp.concatenate([r[0] for r in lf_refs], axis=0)
    n = lf.shape[0]
    l0, l1 = _iota2((LANES, LANES), 0), _iota2((LANES, LANES), 1)
    same_head = l0 % n_heads == l1 % n_heads
    later_in_row = same_head & (l0 // n_heads > l1 // n_heads)
    row_total = _select_sum(lf, same_head, mask_on_left=False)
    later_rows = _iota2((n, n), 1) > _iota2((n, n), 0)
    suffix = (_select_sum(lf, later_in_row, mask_on_left=False)
              + _select_sum(row_total, later_rows, mask_on_left=True)
              + carry_scr[0:1, :])
    carry_scr[...] = carry_scr[...] + _select_sum(row_total, _iota2((SUBLANES, n), 0) >= 0, mask_on_left=True)
    o_ref[0] = (suffix * LOG2E).reshape(o_ref.shape[1:])


def _decode_bias(page_table, lf_cache, n_heads):
    db, n_pages = page_table.shape
    rows = lf_cache.shape[1]
    pb = min(BIAS_PAGES, n_pages)
    groups = n_pages // pb
    assert n_pages % pb == 0

    def page_spec(k):
        return pl.BlockSpec((1, rows, LANES), lambda bi, g, pt: (pt[bi, (groups - 1 - g) * pb + k], 0, 0))

    grid_spec = pltpu.PrefetchScalarGridSpec(
        num_scalar_prefetch=1,
        grid=(db, groups),
        in_specs=[page_spec(k) for k in range(pb)],
        out_specs=pl.BlockSpec((1, pb, rows, LANES), lambda bi, g, pt: (bi, groups - 1 - g, 0, 0)),
        scratch_shapes=[pltpu.VMEM((SUBLANES, LANES), F32)],
    )
    return pl.pallas_call(
        functools.partial(_decode_bias_kernel, n_heads=n_heads),
        grid_spec=grid_spec,
        out_shape=jax.ShapeDtypeStruct((db, n_pages, rows, LANES), F32),
        compiler_params=_params("arbitrary", "arbitrary"),
        name="decode_bias",
    )(page_table, *([lf_cache] * pb))


def _decode_kernel(pt_ref, wt_ref, *refs, n_heads, n_new, n_par, n_steps, n_total):
    (k_hbm, v_hbm, bias_ref, kn_ref, vn_ref, lfc_ref, lfr_ref, sg_ref, o_ref,
     m_scr, l_scr, acc_scr, colb_scr, p_scr, k_buf, v_buf, sems) = refs
    p = pl.program_id(1)
    hs = wt_ref.shape[1]
    n_rows = bias_ref.shape[2]

    g = pl.program_id(0) * n_steps + p
    slot = g % DECODE_SLOTS

    def page_copies(step):
        seq, first = step // n_steps, (step % n_steps) * n_par
        copies = []
        for i in range(n_par):
            page = pt_ref[seq, first + i]
            copies.append(pltpu.make_async_copy(k_hbm.at[page], k_buf.at[step % DECODE_SLOTS, i],
                                                sems.at[0, step % DECODE_SLOTS]))
            copies.append(pltpu.make_async_copy(v_hbm.at[page], v_buf.at[step % DECODE_SLOTS, i],
                                                sems.at[1, step % DECODE_SLOTS]))
        return copies

    @pl.when(g == 0)
    def _prime():
        for step in range(DECODE_SLOTS - 1):
            for copy in page_copies(step):
                copy.start()

    @pl.when(g + DECODE_SLOTS - 1 < n_total)
    def _prefetch():
        for copy in page_copies(g + DECODE_SLOTS - 1):
            copy.start()

    for copy in page_copies(g):
        copy.wait()

    @pl.when(p == 0)
    def _init():
        m_scr[...] = jnp.full(m_scr.shape, NEG_INF, F32)
        l_scr[...] = jnp.zeros(l_scr.shape, F32)
        acc_scr[...] = jnp.zeros(acc_scr.shape, F32)
        i0, i1 = _iota2((hs, hs), 0), _iota2((hs, hs), 1)
        tri = ((i0 // n_new == i1 // n_new) & (i1 % n_new <= i0 % n_new)).astype(F32)
        cum_col = jnp.dot(tri, lfc_ref[0], precision=HIGHEST, preferred_element_type=F32) * LOG2E
        other_head = _iota2((hs, LANES), 0) // n_new != _iota2((hs, LANES), 1) % n_heads
        colb_scr[0] = cum_col
        colb_scr[1] = jnp.where(other_head, NEG_INF, cum_col)

    def softmax_update(st, s_chunks, v_bf16):
        m_old = m_scr[st]
        mx = s_chunks[0]
        for sc in s_chunks[1:]:
            mx = jnp.maximum(mx, sc)
        m_new = jnp.maximum(m_old, jnp.max(mx, axis=-1, keepdims=True))
        alpha = jnp.exp2(m_old - m_new)
        psum = None
        for c, sc in enumerate(s_chunks):
            pc = jnp.exp2(sc - m_new[:, :sc.shape[1]])
            p_scr[st, :, c * LANES:c * LANES + sc.shape[1]] = pc.astype(BF16)
            psum = pc if psum is None else psum + pc
        width = (len(s_chunks) - 1) * LANES + s_chunks[-1].shape[1]
        m_scr[st] = m_new
        l_scr[st] = alpha * l_scr[st] + jnp.sum(psum, axis=-1, keepdims=True)
        pv = jnp.dot(p_scr[st, :, :width], v_bf16, preferred_element_type=F32)
        acc_scr[st] = alpha * acc_scr[st] + pv

    row_bias = colb_scr[1]
    for st in range(n_par):
        kb = k_buf[slot, st].astype(BF16)
        r = lax.dot_general(wt_ref[0], kb, NT_DIMS, preferred_element_type=F32)
        bias = bias_ref[0, st]
        chunks = [r[:, c * LANES:(c + 1) * LANES] + bias[c:c + 1, :] + row_bias for c in range(n_rows)]
        softmax_update(st, chunks, v_buf[slot, st].astype(BF16))

    @pl.when(p == n_steps - 1)
    def _finish():
        sh = kn_ref.shape[1]
        c0, c1 = _iota2((sh, sh), 0), _iota2((sh, sh), 1)
        tri = ((c0 % n_heads == c1 % n_heads) & (c0 // n_heads <= c1 // n_heads)).astype(F32)
        cum_row = jnp.dot(lfr_ref[0], tri, precision=HIGHEST, preferred_element_type=F32) * LOG2E
        rn = lax.dot_general(wt_ref[0], kn_ref[0].astype(BF16), NT_DIMS, preferred_element_type=F32)
        row = _iota2((hs, sh), 0)
        col = _iota2((hs, sh), 1)
        ok = (row // n_new == col % n_heads) & (col // n_heads <= row % n_new)
        sn = jnp.where(ok, rn + colb_scr[0][:, :sh] - cum_row[0:1, :], NEG_INF)
        softmax_update(0, [sn], vn_ref[0].astype(BF16))
        m = m_scr[0]
        for st in range(1, n_par):
            m = jnp.maximum(m, m_scr[st])
        l = jnp.zeros(m.shape, F32)
        acc = jnp.zeros(acc_scr.shape[1:], F32)
        for st in range(n_par):
            w = jnp.exp2(m_scr[st] - m)
            l = l + w * l_scr[st]
            acc = acc + w * acc_scr[st]
        o_ref[0] = (acc / l * sg_ref[0]).astype(o_ref.dtype)


def _decode_attention(page_table, wt, k_cache, v_cache, bias, k_new, v_new, lf_col, lf_row, sg,
                      n_heads, n_new):
    db, n_pages = page_table.shape
    hs, dh = wt.shape[1:]
    n_phys, page_rows, _ = k_cache.shape
    bias_rows = bias.shape[2]
    sh = k_new.shape[1]
    n_par = DECODE_PAGES
    n_steps = n_pages // n_par
    n_total = db * n_steps
    assert dh == LANES and hs <= LANES and LANES % n_heads == 0 and n_pages % n_par == 0
    assert n_total >= DECODE_SLOTS - 1

    def seq(bi, p, pt):
        return (bi, 0, 0)

    grid_spec = pltpu.PrefetchScalarGridSpec(
        num_scalar_prefetch=1,
        grid=(db, n_steps),
        in_specs=[
            pl.BlockSpec((1, hs, dh), seq),
            pl.BlockSpec(memory_space=pl.ANY),
            pl.BlockSpec(memory_space=pl.ANY),
            pl.BlockSpec((1, n_par, bias_rows, LANES), lambda bi, p, pt: (bi, p, 0, 0)),
            pl.BlockSpec((1, sh, dh), seq),
            pl.BlockSpec((1, sh, dh), seq),
            pl.BlockSpec((1, hs, LANES), seq),
            pl.BlockSpec((1, SUBLANES, sh), seq),
            pl.BlockSpec((1, hs, dh), seq),
        ],
        out_specs=pl.BlockSpec((1, hs, dh), seq),
        scratch_shapes=[
            pltpu.VMEM((n_par, hs, LANES), F32),
            pltpu.VMEM((n_par, hs, LANES), F32),
            pltpu.VMEM((n_par, hs, dh), F32),
            pltpu.VMEM((2, hs, LANES), F32),
            pltpu.VMEM((n_par, hs, page_rows), BF16),
            pltpu.VMEM((DECODE_SLOTS, n_par, page_rows, dh), k_cache.dtype),
            pltpu.VMEM((DECODE_SLOTS, n_par, page_rows, dh), v_cache.dtype),
            pltpu.SemaphoreType.DMA((2, DECODE_SLOTS)),
        ],
    )
    return pl.pallas_call(
        functools.partial(_decode_kernel, n_heads=n_heads, n_new=n_new, n_par=n_par, n_steps=n_steps,
                          n_total=n_total),
        grid_spec=grid_spec,
        out_shape=jax.ShapeDtypeStruct((db, hs, dh), BF16),
        compiler_params=_params("arbitrary", "arbitrary"),
        name="decode_attention",
    )(page_table, wt, k_cache, v_cache, bias, k_new, v_new, lf_col, lf_row, sg)


def _conv_kernel(u_ref, hist_ref, w_ref, b_ref, o_ref, win_ref, shift_ref, *, width, chunk):
    tt = u_ref.shape[1]
    n = HIST_ROWS + tt
    pad = HIST_ROWS - (width - 1)

    @pl.when(pl.program_id(2) == 0)
    def _load_history():
        win_ref[0:HIST_ROWS, :] = hist_ref[0]

    win_ref[HIST_ROWS:n, :] = u_ref[0]
    for r in range(1, SUBLANES):
        shift_ref[r, 0:n - SUBLANES, :] = win_ref[r:r + n - SUBLANES, :]
    bias = b_ref[...]
    for rc in range(tt // chunk):
        base = rc * chunk
        acc = jnp.broadcast_to(bias, (chunk, bias.shape[1]))
        for k in range(width):
            a, r = divmod(pad + k, SUBLANES)
            lo = base + a * SUBLANES
            rows = win_ref[lo:lo + chunk, :] if r == 0 else shift_ref[r, lo:lo + chunk, :]
            acc = acc + rows * w_ref[k:k + 1, :]
        o_ref[0, base:base + chunk, :] = acc
    win_ref[0:HIST_ROWS, :] = win_ref[tt:n, :]


def _depthwise_conv(u, hist, w, bias):
    b, t, c = u.shape
    width = w.shape[0]
    tt = min(CONV_ROWS, t)
    tc = min(CONV_COLS, c)
    chunk = min(CONV_CHUNK, tt)
    assert t % tt == 0 and c % tc == 0 and tt % chunk == 0 and width - 1 <= HIST_ROWS
    w_pad = jnp.pad(w, ((0, HIST_ROWS - width), (0, 0)))
    return pl.pallas_call(
        functools.partial(_conv_kernel, width=width, chunk=chunk),
        grid=(b, c // tc, t // tt),
        in_specs=[
            pl.BlockSpec((1, tt, tc), lambda bi, ci, ti: (bi, ti, ci)),
            pl.BlockSpec((1, HIST_ROWS, tc), lambda bi, ci, ti: (bi, 0, ci)),
            pl.BlockSpec((HIST_ROWS, tc), lambda bi, ci, ti: (0, ci)),
            pl.BlockSpec((1, tc), lambda bi, ci, ti: (0, ci)),
        ],
        out_specs=pl.BlockSpec((1, tt, tc), lambda bi, ci, ti: (bi, ti, ci)),
        out_shape=jax.ShapeDtypeStruct((b, t, c), F32),
        scratch_shapes=[pltpu.VMEM((HIST_ROWS + tt, tc), F32),
                        pltpu.VMEM((SUBLANES, HIST_ROWS + tt, tc), F32)],
        compiler_params=_params("arbitrary", "arbitrary", "arbitrary"),
        name="depthwise_conv",
    )(u, hist, w_pad, bias.reshape(1, c))


def _ln_gate_kernel(x_ref, sg_ref, g_ref, b_ref, o_ref):
    x = x_ref[0]
    mu = jnp.mean(x, axis=-1, keepdims=True)
    xc = x - mu
    var = jnp.mean(xc * xc, axis=-1, keepdims=True)
    z = xc * lax.rsqrt(var + LN_EPS) * g_ref[...] + b_ref[...]
    o_ref[0] = (_silu(z) * sg_ref[0]).astype(o_ref.dtype)


def _ln_gate(x, sg, g, bias):
    b, t, c = x.shape
    tr = min(NORM_ROWS, t)
    assert t % tr == 0
    row = pl.BlockSpec((1, tr, c), lambda i, j: (i, j, 0))
    vec = pl.BlockSpec((1, c), lambda i, j: (0, 0))
    return pl.pallas_call(
        _ln_gate_kernel,
        grid=(b, t // tr),
        in_specs=[row, row, vec, vec],
        out_specs=row,
        out_shape=jax.ShapeDtypeStruct((b, t, c), BF16),
        compiler_params=_params("arbitrary", "arbitrary"),
        name="ln_gate",
    )(x, sg, g.reshape(1, c), bias.reshape(1, c))


def kernel(x_prompt, x_sample, c_prompt, c_sample, cache_k, cache_v, cache_logf, state_conv,
           page_table, w_ada, b_ada, norm_g, w_in_attn, b_f, q_gain, k_gain, w_out_attn,
           w_in_conv, dw_w, dw_b, ln_g, ln_b, w_out_conv):
    b, t, d = x_prompt.shape
    db, s, _ = x_sample.shape
    rs = db * s
    depth = w_ada.shape[0]
    n_heads, dh = b_f.shape[1], q_gain.shape[1]
    da = n_heads * dh
    dc = dw_w.shape[2]
    conv_state = dw_w.shape[1] - 1
    page = cache_k.shape[2]
    n_phys = cache_k.shape[1]

    n_c = b + db
    c_all = jnp.pad(jnp.concatenate([c_prompt, c_sample], axis=0), ((0, -n_c % SUBLANES), (0, 0)))
    mod = _ada_mod(c_all, w_ada, b_ada)

    xp = x_prompt
    xs = x_sample.reshape(1, rs, d)
    outs = {name: [] for name in ("kp", "vp", "fp", "ks", "vs", "fs", "cp", "cs")}

    def residual_out(a_p, a_s, w, j, xp, xs, gate_p, gate_s, name):
        tn, tm = COL_TILE // 2, 2 * ROW_TILE
        (xp,), (xs,) = _proj(a_p, a_s, w, j, [0], d, (_epi_residual, _epi_residual),
                             [(xp, _prompt_tile_spec(t, d, tn, tm)), (gate_p, _prompt_vec_spec(d, tn))],
                             [(xs, _sample_tile_spec(rs, d, tn)), (gate_s, _sample_tile_spec(rs, d, tn))],
                             *_plain_outs(b, t, rs, d, F32, tn=tn, tm=tm), name, tn=tn, tm=tm)
        return xp, xs

    for i in range(depth):
        j = i // 2
        mod_p = mod[i, :b][:, None, :]
        mod_s = jnp.repeat(mod[i, b:n_c], s, axis=0)[None]
        hp = _norm_mod(xp, norm_g[i], mod_p[..., d:2 * d], mod_p[..., :d])
        hs = _norm_mod(xs, norm_g[i], mod_s[..., d:2 * d], mod_s[..., :d])
        gate_p, gate_s = mod_p[..., 2 * d:], mod_s[..., 2 * d:]

        if i % 2 == 0:
            w_in_t = jnp.transpose(w_in_attn, (0, 2, 1))

            def in_proj(col, n_cols, epilogues, extra, outs_p, outs_s, name, tn=COL_TILE):
                return _proj(hp, hs, w_in_t, j, [col], n_cols, epilogues, extra, extra, outs_p, outs_s, name,
                             tn=tn, transposed=True)

            hb = min(COL_TILE, da) // dh
            kv_outs_p = [(jax.ShapeDtypeStruct((b, t, da), BF16), _prompt_tile_spec(t, da)),
                         (jax.ShapeDtypeStruct((b, t, n_heads // hb, hb, dh), F32),
                          pl.BlockSpec((1, min(ROW_TILE, t), 1, hb, dh), lambda n, i, m: (i, m, n, 0, 0)))]
            kv_outs_s = _plain_outs(b, t, rs, da, F32)[1]
            q_epi = functools.partial(_epi_heads, dh=dh, scale=dh ** -0.5 * LOG2E)
            k_epi = functools.partial(_epi_heads, dh=dh, scale=1.0)
            v_epi = functools.partial(_epi_heads, dh=dh)
            (q_p,), (q_s,) = in_proj(0, da, (q_epi, q_epi), [(q_gain[j][None], _const_spec((1, dh)))],
                                     *_plain_outs(b, t, rs, da, BF16), "proj_q")
            (kb_p, k_p), (k_s,) = in_proj(da, da, (k_epi, k_epi), [(k_gain[j][None], _const_spec((1, dh)))],
                                          kv_outs_p, kv_outs_s, "proj_k")
            (vb_p, v_p), (v_s,) = in_proj(2 * da, da, (v_epi, v_epi), [], kv_outs_p, kv_outs_s, "proj_v")
            (sg_p,), (sg_s,) = in_proj(3 * da, da, (_epi_silu, _epi_silu), [],
                                       *_plain_outs(b, t, rs, da, BF16), "proj_gate")
            b_f_row = jnp.pad(b_f[j][None], ((0, 0), (0, LANES - n_heads)))
            lf_epi = functools.partial(_epi_log_sigmoid, n_valid=n_heads)
            (lf_p,), (lf_s,) = in_proj(4 * da, LANES, (lf_epi, lf_epi), [(b_f_row, _const_spec((1, LANES)))],
                                       *_plain_outs(b, t, rs, LANES, F32, tn=LANES), "proj_logf", tn=LANES)

            a_p = _prompt_attention(q_p, kb_p, vb_p, _cumsum_t(lf_p), sg_p, n_heads)
            outs["kp"].append(k_p.reshape(b, t, n_heads, dh))
            outs["vp"].append(v_p.reshape(b, t, n_heads, dh))
            outs["fp"].append(lf_p[:, :, :n_heads])

            lf_new = lf_s[0, :, :n_heads].reshape(db, s, n_heads)

            def rows_head_query(x):
                x = x.reshape(db, s, n_heads, dh)
                return jnp.transpose(x, (0, 2, 1, 3)).reshape(db, n_heads * s, dh)

            lf_col = jnp.transpose(lf_new, (0, 2, 1)).reshape(db, n_heads * s, 1)
            lf_col = jnp.broadcast_to(lf_col, (db, n_heads * s, LANES))
            lf_row = jnp.broadcast_to(lf_new.reshape(db, 1, s * n_heads), (db, SUBLANES, s * n_heads))
            bias = _decode_bias(page_table, cache_logf[j].reshape(n_phys, page * n_heads // LANES, LANES), n_heads)
            a_s = _decode_attention(
                page_table, rows_head_query(q_s),
                cache_k[j].reshape(n_phys, page * n_heads, dh),
                cache_v[j].reshape(n_phys, page * n_heads, dh),
                bias, k_s.reshape(db, s * n_heads, dh), v_s.reshape(db, s * n_heads, dh),
                lf_col, lf_row, rows_head_query(sg_s), n_heads, s)
            a_s = jnp.transpose(a_s.reshape(db, n_heads, s, dh), (0, 2, 1, 3)).reshape(1, rs, da)
            outs["ks"].append(k_s.reshape(db, s, n_heads, dh))
            outs["vs"].append(v_s.reshape(db, s, n_heads, dh))
            outs["fs"].append(lf_new)

            xp, xs = residual_out(a_p, a_s, w_out_attn, j, xp, xs, gate_p, gate_s, "out_attn")
        else:
            (sg_p,), (sg_s,) = _proj(hp, hs, w_in_conv, j, [2 * dc], dc, (_epi_silu, _epi_silu), [], [],
                                     *_plain_outs(b, t, rs, dc, BF16), "proj_gate")

            (u_p,), (u_s,) = _proj(hp, hs, w_in_conv, j, [0, dc], dc, (_epi_glu, _epi_glu), [], [],
                                   *_plain_outs(b, t, rs, dc, F32, tn=COL_TILE // 2), "proj_glu", tn=COL_TILE // 2)

            def conv_branch(u, sg, hist):
                rows = u.shape[1]
                pad_rows = -rows % SUBLANES
                u = jnp.pad(u, ((0, 0), (0, pad_rows), (0, 0)))
                sg = jnp.pad(sg, ((0, 0), (0, pad_rows), (0, 0)))
                hist = jnp.pad(hist, ((0, 0), (HIST_ROWS - conv_state, 0), (0, 0)))
                conv = _depthwise_conv(u, hist, dw_w[j], dw_b[j])
                return _ln_gate(conv, sg, ln_g[j], ln_b[j])[:, :rows]

            a_p = conv_branch(u_p, sg_p, jnp.zeros((b, conv_state, dc), F32))
            outs["cp"].append(u_p[:, t - conv_state:, :])
            u_s = u_s.reshape(db, s, dc)
            a_s = conv_branch(u_s, sg_s.reshape(db, s, dc), state_conv[j]).reshape(1, rs, dc)
            outs["cs"].append(jnp.concatenate([state_conv[j], u_s], axis=1)[:, -conv_state:, :])

            xp, xs = residual_out(a_p, a_s, w_out_conv, j, xp, xs, gate_p, gate_s, "out_conv")

    return (xp, xs.reshape(db, s, d), jnp.stack(outs["kp"]), jnp.stack(outs["vp"]), jnp.stack(outs["fp"]),
            jnp.stack(outs["ks"]), jnp.stack(outs["vs"]), jnp.stack(outs["fs"]),
            jnp.stack(outs["cp"]), jnp.stack(outs["cs"]))
```

```python
import functools
import math

import jax
import jax.numpy as jnp
from jax import lax
from jax.experimental import pallas as pl
from jax.experimental.pallas import tpu as pltpu

F32 = jnp.float32
BF16 = jnp.bfloat16
HIGHEST = lax.Precision.HIGHEST

RMS_EPS = 1e-6
LN_EPS = 1e-5
NEG_INF = -1e30
LOG2E = math.log2(math.e)

LANES = 128
SUBLANES = 8
VMEM_LIMIT_BYTES = 63 * 1024 * 1024

ROW_TILE = 512
COL_TILE = 1024
ADA_COLS = 512
ATTN_TILE = 512
NORM_ROWS = 256
CONV_ROWS = 256
CONV_COLS = 512
CONV_CHUNK = 32
HIST_ROWS = 32
CUMSUM_CHUNK = 256
BIAS_PAGES = 16
DECODE_PAGES = 4
DECODE_SLOTS = 3
ATTN_HALVES = 4

NT_DIMS = (((1,), (1,)), ((), ()))


def _params(*sem):
    return pltpu.CompilerParams(dimension_semantics=sem, vmem_limit_bytes=VMEM_LIMIT_BYTES)


def _silu(x):
    return x * jax.nn.sigmoid(x)


def _iota2(shape, dim):
    return lax.broadcasted_iota(jnp.int32, shape, dim)


def _ada_kernel(c_ref, w_ref, b_ref, o_ref):
    sc = _silu(c_ref[...]).astype(BF16)
    w = w_ref[0].astype(BF16)
    o_ref[0] = jnp.dot(sc, w, preferred_element_type=F32) + b_ref[0]


def _ada_mod(c, w_ada, b_ada):
    depth, d, n3 = w_ada.shape
    r = c.shape[0]
    tn = min(ADA_COLS, n3)
    assert n3 % tn == 0 and r % SUBLANES == 0
    return pl.pallas_call(
        _ada_kernel,
        grid=(depth, n3 // tn),
        in_specs=[
            pl.BlockSpec((r, d), lambda i, n: (0, 0)),
            pl.BlockSpec((1, d, tn), lambda i, n: (i, 0, n)),
            pl.BlockSpec((1, 1, tn), lambda i, n: (i, 0, n)),
        ],
        out_specs=pl.BlockSpec((1, r, tn), lambda i, n: (i, 0, n)),
        out_shape=jax.ShapeDtypeStruct((depth, r, n3), F32),
        compiler_params=_params("arbitrary", "arbitrary"),
        name="ada_mod",
    )(c, w_ada, b_ada.reshape(depth, 1, n3))


def _norm_mod_kernel(x_ref, g_ref, sc_ref, sh_ref, o_ref):
    x = x_ref[0]
    ms = jnp.mean(x * x, axis=-1, keepdims=True)
    y = x * lax.rsqrt(ms + RMS_EPS) * g_ref[...]
    o_ref[0] = (y * (1.0 + sc_ref[0]) + sh_ref[0]).astype(o_ref.dtype)


def _norm_mod(x, g, scale, shift):
    b, t, d = x.shape
    tr = min(NORM_ROWS, t)
    assert t % tr == 0
    per_row = scale.shape[1] != 1
    mod_spec = (pl.BlockSpec((1, tr, d), lambda i, j: (i, j, 0)) if per_row
                else pl.BlockSpec((1, 1, d), lambda i, j: (i, 0, 0)))
    return pl.pallas_call(
        _norm_mod_kernel,
        grid=(b, t // tr),
        in_specs=[
            pl.BlockSpec((1, tr, d), lambda i, j: (i, j, 0)),
            pl.BlockSpec((1, d), lambda i, j: (0, 0)),
            mod_spec, mod_spec,
        ],
        out_specs=pl.BlockSpec((1, tr, d), lambda i, j: (i, j, 0)),
        out_shape=jax.ShapeDtypeStruct((b, t, d), BF16),
        compiler_params=_params("arbitrary", "arbitrary"),
        name="norm_mod",
    )(x, g.reshape(1, d), scale, shift)


def _proj_kernel(*refs, n_w, n_extras, n_outs, epilogues, transposed):
    a_refs = refs[:2]
    w_refs = refs[2:2 + n_w]
    pos = 2 + n_w
    extras = (refs[pos:pos + n_extras], refs[pos + n_extras:pos + 2 * n_extras])
    pos += 2 * n_extras
    out_refs = (refs[pos:pos + n_outs[0]], refs[pos + n_outs[0]:pos + sum(n_outs)])
    w_scrs = refs[pos + sum(n_outs):]
    first_rows = (pl.program_id(1) == 0) & (pl.program_id(2) == 0)

    def products(a):
        if transposed:
            return [lax.dot_general(a, scr[...], NT_DIMS, preferred_element_type=F32) for scr in w_scrs]
        return [jnp.dot(a, scr[...], preferred_element_type=F32) for scr in w_scrs]

    @pl.when(first_rows)
    def _new_columns():
        for w_ref, scr in zip(w_refs, w_scrs):
            scr[...] = w_ref[0].astype(BF16)
        tm = a_refs[0].shape[1]
        accs = products(jnp.concatenate([a_refs[0][0], a_refs[1][0]], axis=0))
        epilogues[0]([acc[:tm] for acc in accs], extras[0], out_refs[0])
        epilogues[1]([acc[tm:] for acc in accs], extras[1], out_refs[1])

    @pl.when(jnp.logical_not(first_rows))
    def _same_columns():
        epilogues[0](products(a_refs[0][0]), extras[0], out_refs[0])


def _proj(a_p, a_s, w, layer, col_offsets, n_cols, epilogues, extras_p, extras_s, outs_p, outs_s, name,
          tn=COL_TILE, tm=ROW_TILE, transposed=False):
    b, t, k = a_p.shape
    r = a_s.shape[1]
    tm = min(tm, t)
    tn = min(tn, n_cols)
    assert t % tm == 0 and n_cols % tn == 0 and all(off % tn == 0 for off in col_offsets)
    assert len(extras_p) == len(extras_s)
    in_specs = [pl.BlockSpec((1, tm, k), lambda n, i, m: (i, m, 0)),
                pl.BlockSpec((1, r, k), lambda n, i, m: (0, 0, 0))]
    for off in col_offsets:
        if transposed:
            in_specs.append(pl.BlockSpec((1, tn, k), lambda n, i, m, ob=off // tn: (layer, ob + n, 0)))
        else:
            in_specs.append(pl.BlockSpec((1, k, tn), lambda n, i, m, ob=off // tn: (layer, 0, ob + n)))
    in_specs += [spec for _, spec in extras_p] + [spec for _, spec in extras_s]
    outs = list(outs_p) + list(outs_s)
    results = pl.pallas_call(
        functools.partial(_proj_kernel, n_w=len(col_offsets), n_extras=len(extras_p),
                          n_outs=(len(outs_p), len(outs_s)), epilogues=epilogues, transposed=transposed),
        grid=(n_cols // tn, b, t // tm),
        in_specs=in_specs,
        out_specs=[spec for _, spec in outs],
        out_shape=[sds for sds, _ in outs],
        scratch_shapes=[pltpu.VMEM((tn, k) if transposed else (k, tn), BF16) for _ in col_offsets],
        compiler_params=_params("arbitrary", "arbitrary", "arbitrary"),
        name=name,
    )(a_p, a_s, *([w] * len(col_offsets)), *[arr for arr, _ in extras_p], *[arr for arr, _ in extras_s])
    return results[:len(outs_p)], results[len(outs_p):]


def _plain_outs(b, t, r, n_cols, dtype, tn=COL_TILE, tm=ROW_TILE):
    return ([(jax.ShapeDtypeStruct((b, t, n_cols), dtype), _prompt_tile_spec(t, n_cols, tn, tm))],
            [(jax.ShapeDtypeStruct((1, r, n_cols), dtype), _sample_tile_spec(r, n_cols, tn))])


def _prompt_tile_spec(t, n_cols, tn=COL_TILE, tm=ROW_TILE):
    return pl.BlockSpec((1, min(tm, t), min(tn, n_cols)), lambda n, i, m: (i, m, n))


def _prompt_vec_spec(n_cols, tn=COL_TILE):
    return pl.BlockSpec((1, 1, min(tn, n_cols)), lambda n, i, m: (i, 0, n))


def _sample_tile_spec(r, n_cols, tn=COL_TILE):
    return pl.BlockSpec((1, r, min(tn, n_cols)), lambda n, i, m: (0, 0, n))


def _const_spec(shape):
    return pl.BlockSpec(shape, lambda n, i, m: (0,) * len(shape))


def _epi_silu(accs, extra, outs):
    outs[0][0] = _silu(accs[0]).astype(outs[0].dtype)


def _epi_glu(accs, extra, outs):
    outs[0][0] = (accs[0] * jax.nn.sigmoid(accs[1])).astype(outs[0].dtype)


def _epi_residual(accs, extra, outs):
    x_ref, gate_ref = extra
    outs[0][0] = x_ref[0] + gate_ref[0] * accs[0]


def _epi_log_sigmoid(accs, extra, outs, *, n_valid):
    x = accs[0] + extra[0][...]
    y = jnp.minimum(x, 0.0) - jnp.log1p(jnp.exp(-jnp.abs(x)))
    outs[0][0] = jnp.where(_iota2(y.shape, 1) < n_valid, y, 0.0)


def _epi_heads(accs, extra, outs, *, dh, scale=None):
    acc = accs[0]
    rows, heads = acc.shape[0], acc.shape[1] // dh
    by_head = outs[1].reshape(rows * heads, dh) if len(outs) > 1 else None
    for j in range(heads):
        blk = acc[:, j * dh:(j + 1) * dh]
        if scale is not None:
            ms = jnp.mean(blk * blk, axis=-1, keepdims=True)
            blk = blk * lax.rsqrt(ms + RMS_EPS) * (extra[0][...] * scale)
        outs[0][0, :, j * dh:(j + 1) * dh] = blk.astype(outs[0].dtype)
        if by_head is not None:
            by_head[pl.ds(j, rows, stride=heads), :] = blk


def _cumsum_kernel(x_ref, o_ref):
    t = x_ref.shape[1]
    ch = min(CUMSUM_CHUNK, t)
    tri = (_iota2((ch, ch), 1) <= _iota2((ch, ch), 0)).astype(F32)
    carry = jnp.zeros((1, x_ref.shape[2]), F32)
    for i in range(t // ch):
        blk = x_ref[0, i * ch:(i + 1) * ch, :]
        cs = jnp.dot(tri, blk, precision=HIGHEST, preferred_element_type=F32) + carry
        o_ref[0, i * ch:(i + 1) * ch, :] = cs
        carry = cs[ch - 1:ch, :]


def _cumsum_t(x):
    b, t, n = x.shape
    return pl.pallas_call(
        _cumsum_kernel,
        grid=(b,),
        in_specs=[pl.BlockSpec((1, t, n), lambda i: (i, 0, 0))],
        out_specs=pl.BlockSpec((1, t, n), lambda i: (i, 0, 0)),
        out_shape=jax.ShapeDtypeStruct((b, t, n), F32),
        compiler_params=_params("arbitrary"),
        name="cumsum_t",
    )(x)


def _attn_kernel(q_ref, k_ref, v_ref, cq_ref, ck_ref, sg_ref, o_ref, vb_scr, m_scr, acc_scr,
                 *, tile, n_q, n_groups):
    h = pl.program_id(1)
    i = pl.program_id(2)
    dh = q_ref.shape[2]

    @pl.when(i == 0)
    def _augment_v():
        vb_scr[:, :dh] = v_ref[0]
        vb_scr[:, dh:] = jnp.where(_iota2((vb_scr.shape[0], dh), 1) == 0, 1.0, 0.0).astype(BF16)

    rows = [slice(a * tile, (a + 1) * tile) for a in range(n_q)]
    qs = [q_ref[0, r, :] for r in rows]
    lane = _iota2((tile, cq_ref.shape[2]), 1)
    cqs = [jnp.sum(jnp.where(lane == h, cq_ref[0, r, :], 0.0), axis=-1, keepdims=True) * LOG2E for r in rows]

    m_scr[...] = jnp.full(m_scr.shape, NEG_INF, F32)
    acc_scr[...] = jnp.zeros(acc_scr.shape, F32)

    def step(a, j, masked):
        start = j * tile
        s = lax.dot_general(qs[a], k_ref[0, pl.ds(start, tile), :], NT_DIMS, preferred_element_type=F32)
        s = s + cqs[a] - ck_ref[0, 0, j] * LOG2E
        if masked:
            s = jnp.where(_iota2(s.shape, 1) <= _iota2(s.shape, 0), s, NEG_INF)
        m_old = m_scr[a]
        m_new = jnp.maximum(m_old, jnp.max(s, axis=-1, keepdims=True))
        p = jnp.exp2(s - jnp.concatenate([m_new] * (tile // LANES), axis=1)).astype(BF16)
        pv = jnp.dot(p, vb_scr[pl.ds(start, tile), :], preferred_element_type=F32)
        alpha = jnp.exp2(m_old - m_new)
        acc_scr[a] = jnp.concatenate([alpha] * (2 * dh // LANES), axis=1) * acc_scr[a] + pv
        m_scr[a] = m_new

    def run_group(group):
        first = group * n_q
        for j in range(first):
            for a in range(n_q):
                step(a, j, False)
        for jj in range(n_q):
            for a in range(jj, n_q):
                step(a, first + jj, a == jj)
        for a, r in enumerate(rows):
            acc = acc_scr[a]
            o_ref[0, r, :] = (acc[:, :dh] / acc[:, dh:dh + 1] * sg_ref[0, r, :]).astype(o_ref.dtype)

    for group in range(n_groups):
        pl.when(i == group)(functools.partial(run_group, group))


def _prompt_attention(q, k, v, cum, sg, n_heads):
    b, t, da = q.shape
    dh = da // n_heads
    tile = min(ATTN_TILE, t)
    n_q = min(ATTN_HALVES, t // tile)
    group = n_q * tile
    assert t % group == 0 and dh % LANES == 0
    cum_t = jnp.transpose(cum[:, :, :n_heads], (0, 2, 1)).reshape(b, n_heads, t // tile, 1, tile)
    return pl.pallas_call(
        functools.partial(_attn_kernel, tile=tile, n_q=n_q, n_groups=t // group),
        grid=(b, n_heads, t // group),
        in_specs=[
            pl.BlockSpec((1, group, dh), lambda bi, h, i: (bi, i, h)),
            pl.BlockSpec((1, t, dh), lambda bi, h, i: (bi, 0, h)),
            pl.BlockSpec((1, t, dh), lambda bi, h, i: (bi, 0, h)),
            pl.BlockSpec((1, group, cum.shape[2]), lambda bi, h, i: (bi, i, 0)),
            pl.BlockSpec((1, 1, t // tile, 1, tile), lambda bi, h, i: (bi, h, 0, 0, 0)),
            pl.BlockSpec((1, group, dh), lambda bi, h, i: (bi, i, h)),
        ],
        out_specs=pl.BlockSpec((1, group, dh), lambda bi, h, i: (bi, i, h)),
        out_shape=jax.ShapeDtypeStruct((b, t, da), BF16),
        scratch_shapes=[pltpu.VMEM((t, 2 * dh), BF16),
                        pltpu.VMEM((n_q, tile, LANES), F32), pltpu.VMEM((n_q, tile, 2 * dh), F32)],
        compiler_params=_params("arbitrary", "arbitrary", "arbitrary"),
        name="prompt_attention",
    )(q, k, v, cum, cum_t, sg)


def _select_sum(x, mask, mask_on_left):
    hi = x.astype(BF16)
    rest = x - hi.astype(F32)
    mid = rest.astype(BF16)
    lo = (rest - mid.astype(F32)).astype(BF16)
    m = jnp.where(mask, 1.0, 0.0).astype(BF16)
    total = None
    for part in (hi, mid, lo):
        prod = (jnp.dot(m, part, preferred_element_type=F32) if mask_on_left
                else jnp.dot(part, m, preferred_element_type=F32))
        total = prod if total is None else total + prod
    return total


def _decode_bias_kernel(pt_ref, *refs, n_heads):
    lf_refs, o_ref, carry_scr = refs[:-2], refs[-2], refs[-1]

    @pl.when(pl.program_id(1) == 0)
    def _init():
        carry_scr[...] = jnp.zeros(carry_scr.shape, F32)

    lf = jnp.concatenate([r[0] for r in lf_refs], axis=0)
    n = lf.shape[0]
    l0, l1 = _iota2((LANES, LANES), 0), _iota2((LANES, LANES), 1)
    same_head = l0 % n_heads == l1 % n_heads
    later_in_row = same_head & (l0 // n_heads > l1 // n_heads)
    row_total = _select_sum(lf, same_head, mask_on_left=False)
    later_rows = _iota2((n, n), 1) > _iota2((n, n), 0)
    suffix = (_select_sum(lf, later_in_row, mask_on_left=False)
              + _select_sum(row_total, later_rows, mask_on_left=True)
              + carry_scr[0:1, :])
    carry_scr[...] = carry_scr[...] + _select_sum(row_total, _iota2((SUBLANES, n), 0) >= 0, mask_on_left=True)
    o_ref[0] = (suffix * LOG2E).reshape(o_ref.shape[1:])


def _decode_bias(page_table, lf_cache, n_heads):
    db, n_pages = page_table.shape
    rows = lf_cache.shape[1]
    pb = min(BIAS_PAGES, n_pages)
    groups = n_pages // pb
    assert n_pages % pb == 0

    def page_spec(k):
        return pl.BlockSpec((1, rows, LANES), lambda bi, g, pt: (pt[bi, (groups - 1 - g) * pb + k], 0, 0))

    grid_spec = pltpu.PrefetchScalarGridSpec(
        num_scalar_prefetch=1,
        grid=(db, groups),
        in_specs=[page_spec(k) for k in range(pb)],
        out_specs=pl.BlockSpec((1, pb, rows, LANES), lambda bi, g, pt: (bi, groups - 1 - g, 0, 0)),
        scratch_shapes=[pltpu.VMEM((SUBLANES, LANES), F32)],
    )
    return pl.pallas_call(
        functools.partial(_decode_bias_kernel, n_heads=n_heads),
        grid_spec=grid_spec,
        out_shape=jax.ShapeDtypeStruct((db, n_pages, rows, LANES), F32),
        compiler_params=_params("arbitrary", "arbitrary"),
        name="decode_bias",
    )(page_table, *([lf_cache] * pb))


def _decode_kernel(pt_ref, wt_ref, *refs, n_heads, n_new, n_par, n_steps, n_total):
    (k_hbm, v_hbm, bias_ref, kn_ref, vn_ref, lfc_ref, lfr_ref, sg_ref, o_ref,
     m_scr, l_scr, acc_scr, colb_scr, p_scr, k_buf, v_buf, sems) = refs
    p = pl.program_id(1)
    hs = wt_ref.shape[1]
    n_rows = bias_ref.shape[2]

    g = pl.program_id(0) * n_steps + p
    slot = g % DECODE_SLOTS

    def page_copies(step):
        seq, first = step // n_steps, (step % n_steps) * n_par
        copies = []
        for i in range(n_par):
            page = pt_ref[seq, first + i]
            copies.append(pltpu.make_async_copy(k_hbm.at[page], k_buf.at[step % DECODE_SLOTS, i],
                                                sems.at[0, step % DECODE_SLOTS]))
            copies.append(pltpu.make_async_copy(v_hbm.at[page], v_buf.at[step % DECODE_SLOTS, i],
                                                sems.at[1, step % DECODE_SLOTS]))
        return copies

    @pl.when(g == 0)
    def _prime():
        for step in range(DECODE_SLOTS - 1):
            for copy in page_copies(step):
                copy.start()

    @pl.when(g + DECODE_SLOTS - 1 < n_total)
    def _prefetch():
        for copy in page_copies(g + DECODE_SLOTS - 1):
            copy.start()

    for copy in page_copies(g):
        copy.wait()

    @pl.when(p == 0)
    def _init():
        m_scr[...] = jnp.full(m_scr.shape, NEG_INF, F32)
        l_scr[...] = jnp.zeros(l_scr.shape, F32)
        acc_scr[...] = jnp.zeros(acc_scr.shape, F32)
        i0, i1 = _iota2((hs, hs), 0), _iota2((hs, hs), 1)
        tri = ((i0 // n_new == i1 // n_new) & (i1 % n_new <= i0 % n_new)).astype(F32)
        cum_col = jnp.dot(tri, lfc_ref[0], precision=HIGHEST, preferred_element_type=F32) * LOG2E
        other_head = _iota2((hs, LANES), 0) // n_new != _iota2((hs, LANES), 1) % n_heads
        colb_scr[0] = cum_col
        colb_scr[1] = jnp.where(other_head, NEG_INF, cum_col)

    def softmax_update(st, s_chunks, v_bf16):
        m_old = m_scr[st]
        mx = s_chunks[0]
        for sc in s_chunks[1:]:
            mx = jnp.maximum(mx, sc)
        m_new = jnp.maximum(m_old, jnp.max(mx, axis=-1, keepdims=True))
        alpha = jnp.exp2(m_old - m_new)
        psum = None
        for c, sc in enumerate(s_chunks):
            pc = jnp.exp2(sc - m_new[:, :sc.shape[1]])
            p_scr[st, :, c * LANES:c * LANES + sc.shape[1]] = pc.astype(BF16)
            psum = pc if psum is None else psum + pc
        width = (len(s_chunks) - 1) * LANES + s_chunks[-1].shape[1]
        m_scr[st] = m_new
        l_scr[st] = alpha * l_scr[st] + jnp.sum(psum, axis=-1, keepdims=True)
        pv = jnp.dot(p_scr[st, :, :width], v_bf16, preferred_element_type=F32)
        acc_scr[st] = alpha * acc_scr[st] + pv

    row_bias = colb_scr[1]
    for st in range(n_par):
        kb = k_buf[slot, st].astype(BF16)
        r = lax.dot_general(wt_ref[0], kb, NT_DIMS, preferred_element_type=F32)
        bias = bias_ref[0, st]
        chunks = [r[:, c * LANES:(c + 1) * LANES] + bias[c:c + 1, :] + row_bias for c in range(n_rows)]
        softmax_update(st, chunks, v_buf[slot, st].astype(BF16))

    @pl.when(p == n_steps - 1)
    def _finish():
        sh = kn_ref.shape[1]
        c0, c1 = _iota2((sh, sh), 0), _iota2((sh, sh), 1)
        tri = ((c0 % n_heads == c1 % n_heads) & (c0 // n_heads <= c1 // n_heads)).astype(F32)
        cum_row = jnp.dot(lfr_ref[0], tri, precision=HIGHEST, preferred_element_type=F32) * LOG2E
        rn = lax.dot_general(wt_ref[0], kn_ref[0].astype(BF16), NT_DIMS, preferred_element_type=F32)
        row = _iota2((hs, sh), 0)
        col = _iota2((hs, sh), 1)
        ok = (row // n_new == col % n_heads) & (col // n_heads <= row % n_new)
        sn = jnp.where(ok, rn + colb_scr[0][:, :sh] - cum_row[0:1, :], NEG_INF)
        softmax_update(0, [sn], vn_ref[0].astype(BF16))
        m = m_scr[0]
        for st in range(1, n_par):
            m = jnp.maximum(m, m_scr[st])
        l = jnp.zeros(m.shape, F32)
        acc = jnp.zeros(acc_scr.shape[1:], F32)
        for st in range(n_par):
            w = jnp.exp2(m_scr[st] - m)
            l = l + w * l_scr[st]
            acc = acc + w * acc_scr[st]
        o_ref[0] = (acc / l * sg_ref[0]).astype(o_ref.dtype)


def _decode_attention(page_table, wt, k_cache, v_cache, bias, k_new, v_new, lf_col, lf_row, sg,
                      n_heads, n_new):
    db, n_pages = page_table.shape
    hs, dh = wt.shape[1:]
    n_phys, page_rows, _ = k_cache.shape
    bias_rows = bias.shape[2]
    sh = k_new.shape[1]
    n_par = DECODE_PAGES
    n_steps = n_pages // n_par
    n_total = db * n_steps
    assert dh == LANES and hs <= LANES and LANES % n_heads == 0 and n_pages % n_par == 0
    assert n_total >= DECODE_SLOTS - 1

    def seq(bi, p, pt):
        return (bi, 0, 0)

    grid_spec = pltpu.PrefetchScalarGridSpec(
        num_scalar_prefetch=1,
        grid=(db, n_steps),
        in_specs=[
            pl.BlockSpec((1, hs, dh), seq),
            pl.BlockSpec(memory_space=pl.ANY),
            pl.BlockSpec(memory_space=pl.ANY),
            pl.BlockSpec((1, n_par, bias_rows, LANES), lambda bi, p, pt: (bi, p, 0, 0)),
            pl.BlockSpec((1, sh, dh), seq),
            pl.BlockSpec((1, sh, dh), seq),
            pl.BlockSpec((1, hs, LANES), seq),
            pl.BlockSpec((1, SUBLANES, sh), seq),
            pl.BlockSpec((1, hs, dh), seq),
        ],
        out_specs=pl.BlockSpec((1, hs, dh), seq),
        scratch_shapes=[
            pltpu.VMEM((n_par, hs, LANES), F32),
            pltpu.VMEM((n_par, hs, LANES), F32),
            pltpu.VMEM((n_par, hs, dh), F32),
            pltpu.VMEM((2, hs, LANES), F32),
            pltpu.VMEM((n_par, hs, page_rows), BF16),
            pltpu.VMEM((DECODE_SLOTS, n_par, page_rows, dh), k_cache.dtype),
            pltpu.VMEM((DECODE_SLOTS, n_par, page_rows, dh), v_cache.dtype),
            pltpu.SemaphoreType.DMA((2, DECODE_SLOTS)),
        ],
    )
    return pl.pallas_call(
        functools.partial(_decode_kernel, n_heads=n_heads, n_new=n_new, n_par=n_par, n_steps=n_steps,
                          n_total=n_total),
        grid_spec=grid_spec,
        out_shape=jax.ShapeDtypeStruct((db, hs, dh), BF16),
        compiler_params=_params("arbitrary", "arbitrary"),
        name="decode_attention",
    )(page_table, wt, k_cache, v_cache, bias, k_new, v_new, lf_col, lf_row, sg)


def _conv_kernel(u_ref, hist_ref, w_ref, b_ref, o_ref, win_ref, shift_ref, *, width, chunk):
    tt = u_ref.shape[1]
    n = HIST_ROWS + tt
    pad = HIST_ROWS - (width - 1)

    @pl.when(pl.program_id(2) == 0)
    def _load_history():
        win_ref[0:HIST_ROWS, :] = hist_ref[0]

    win_ref[HIST_ROWS:n, :] = u_ref[0]
    for r in range(1, SUBLANES):
        shift_ref[r, 0:n - SUBLANES, :] = win_ref[r:r + n - SUBLANES, :]
    bias = b_ref[...]
    for rc in range(tt // chunk):
        base = rc * chunk
        acc = jnp.broadcast_to(bias, (chunk, bias.shape[1]))
        for k in range(width):
            a, r = divmod(pad + k, SUBLANES)
            lo = base + a * SUBLANES
            rows = win_ref[lo:lo + chunk, :] if r == 0 else shift_ref[r, lo:lo + chunk, :]
            acc = acc + rows * w_ref[k:k + 1, :]
        o_ref[0, base:base + chunk, :] = acc
    win_ref[0:HIST_ROWS, :] = win_ref[tt:n, :]


def _depthwise_conv(u, hist, w, bias):
    b, t, c = u.shape
    width = w.shape[0]
    tt = min(CONV_ROWS, t)
    tc = min(CONV_COLS, c)
    chunk = min(CONV_CHUNK, tt)
    assert t % tt == 0 and c % tc == 0 and tt % chunk == 0 and width - 1 <= HIST_ROWS
    w_pad = jnp.pad(w, ((0, HIST_ROWS - width), (0, 0)))
    return pl.pallas_call(
        functools.partial(_conv_kernel, width=width, chunk=chunk),
        grid=(b, c // tc, t // tt),
        in_specs=[
            pl.BlockSpec((1, tt, tc), lambda bi, ci, ti: (bi, ti, ci)),
            pl.BlockSpec((1, HIST_ROWS, tc), lambda bi, ci, ti: (bi, 0, ci)),
            pl.BlockSpec((HIST_ROWS, tc), lambda bi, ci, ti: (0, ci)),
            pl.BlockSpec((1, tc), lambda bi, ci, ti: (0, ci)),
        ],
        out_specs=pl.BlockSpec((1, tt, tc), lambda bi, ci, ti: (bi, ti, ci)),
        out_shape=jax.ShapeDtypeStruct((b, t, c), F32),
        scratch_shapes=[pltpu.VMEM((HIST_ROWS + tt, tc), F32),
                        pltpu.VMEM((SUBLANES, HIST_ROWS + tt, tc), F32)],
        compiler_params=_params("arbitrary", "arbitrary", "arbitrary"),
        name="depthwise_conv",
    )(u, hist, w_pad, bias.reshape(1, c))


def _ln_gate_kernel(x_ref, sg_ref, g_ref, b_ref, o_ref):
    x = x_ref[0]
    mu = jnp.mean(x, axis=-1, keepdims=True)
    xc = x - mu
    var = jnp.mean(xc * xc, axis=-1, keepdims=True)
    z = xc * lax.rsqrt(var + LN_EPS) * g_ref[...] + b_ref[...]
    o_ref[0] = (_silu(z) * sg_ref[0]).astype(o_ref.dtype)


def _ln_gate(x, sg, g, bias):
    b, t, c = x.shape
    tr = min(NORM_ROWS, t)
    assert t % tr == 0
    row = pl.BlockSpec((1, tr, c), lambda i, j: (i, j, 0))
    vec = pl.BlockSpec((1, c), lambda i, j: (0, 0))
    return pl.pallas_call(
        _ln_gate_kernel,
        grid=(b, t // tr),
        in_specs=[row, row, vec, vec],
        out_specs=row,
        out_shape=jax.ShapeDtypeStruct((b, t, c), BF16),
        compiler_params=_params("arbitrary", "arbitrary"),
        name="ln_gate",
    )(x, sg, g.reshape(1, c), bias.reshape(1, c))


def kernel(x_prompt, x_sample, c_prompt, c_sample, cache_k, cache_v, cache_logf, state_conv,
           page_table, w_ada, b_ada, norm_g, w_in_attn, b_f, q_gain, k_gain, w_out_attn,
           w_in_conv, dw_w, dw_b, ln_g, ln_b, w_out_conv):
    b, t, d = x_prompt.shape
    db, s, _ = x_sample.shape
    rs = db * s
    depth = w_ada.shape[0]
    n_heads, dh = b_f.shape[1], q_gain.shape[1]
    da = n_heads * dh
    dc = dw_w.shape[2]
    conv_state = dw_w.shape[1] - 1
    page = cache_k.shape[2]
    n_phys = cache_k.shape[1]

    n_c = b + db
    c_all = jnp.pad(jnp.concatenate([c_prompt, c_sample], axis=0), ((0, -n_c % SUBLANES), (0, 0)))
    mod = _ada_mod(c_all, w_ada, b_ada)

    xp = x_prompt
    xs = x_sample.reshape(1, rs, d)
    outs = {name: [] for name in ("kp", "vp", "fp", "ks", "vs", "fs", "cp", "cs")}

    def residual_out(a_p, a_s, w, j, xp, xs, gate_p, gate_s, name):
        tn, tm = COL_TILE // 2, 2 * ROW_TILE
        (xp,), (xs,) = _proj(a_p, a_s, w, j, [0], d, (_epi_residual, _epi_residual),
                             [(xp, _prompt_tile_spec(t, d, tn, tm)), (gate_p, _prompt_vec_spec(d, tn))],
                             [(xs, _sample_tile_spec(rs, d, tn)), (gate_s, _sample_tile_spec(rs, d, tn))],
                             *_plain_outs(b, t, rs, d, F32, tn=tn, tm=tm), name, tn=tn, tm=tm)
        return xp, xs

    for i in range(depth):
        j = i // 2
        mod_p = mod[i, :b][:, None, :]
        mod_s = jnp.repeat(mod[i, b:n_c], s, axis=0)[None]
        hp = _norm_mod(xp, norm_g[i], mod_p[..., d:2 * d], mod_p[..., :d])
        hs = _norm_mod(xs, norm_g[i], mod_s[..., d:2 * d], mod_s[..., :d])
        gate_p, gate_s = mod_p[..., 2 * d:], mod_s[..., 2 * d:]

        if i % 2 == 0:
            w_in_t = jnp.transpose(w_in_attn, (0, 2, 1))

            def in_proj(col, n_cols, epilogues, extra, outs_p, outs_s, name, tn=COL_TILE):
                return _proj(hp, hs, w_in_t, j, [col], n_cols, epilogues, extra, extra, outs_p, outs_s, name,
                             tn=tn, transposed=True)

            hb = min(COL_TILE, da) // dh
            kv_outs_p = [(jax.ShapeDtypeStruct((b, t, da), BF16), _prompt_tile_spec(t, da)),
                         (jax.ShapeDtypeStruct((b, t, n_heads // hb, hb, dh), F32),
                          pl.BlockSpec((1, min(ROW_TILE, t), 1, hb, dh), lambda n, i, m: (i, m, n, 0, 0)))]
            kv_outs_s = _plain_outs(b, t, rs, da, F32)[1]
            q_epi = functools.partial(_epi_heads, dh=dh, scale=dh ** -0.5 * LOG2E)
            k_epi = functools.partial(_epi_heads, dh=dh, scale=1.0)
            v_epi = functools.partial(_epi_heads, dh=dh)
            (q_p,), (q_s,) = in_proj(0, da, (q_epi, q_epi), [(q_gain[j][None], _const_spec((1, dh)))],
                                     *_plain_outs(b, t, rs, da, BF16), "proj_q")
            (kb_p, k_p), (k_s,) = in_proj(da, da, (k_epi, k_epi), [(k_gain[j][None], _const_spec((1, dh)))],
                                          kv_outs_p, kv_outs_s, "proj_k")
            (vb_p, v_p), (v_s,) = in_proj(2 * da, da, (v_epi, v_epi), [], kv_outs_p, kv_outs_s, "proj_v")
            (sg_p,), (sg_s,) = in_proj(3 * da, da, (_epi_silu, _epi_silu), [],
                                       *_plain_outs(b, t, rs, da, BF16), "proj_gate")
            b_f_row = jnp.pad(b_f[j][None], ((0, 0), (0, LANES - n_heads)))
            lf_epi = functools.partial(_epi_log_sigmoid, n_valid=n_heads)
            (lf_p,), (lf_s,) = in_proj(4 * da, LANES, (lf_epi, lf_epi), [(b_f_row, _const_spec((1, LANES)))],
                                       *_plain_outs(b, t, rs, LANES, F32, tn=LANES), "proj_logf", tn=LANES)

            a_p = _prompt_attention(q_p, kb_p, vb_p, _cumsum_t(lf_p), sg_p, n_heads)
            outs["kp"].append(k_p.reshape(b, t, n_heads, dh))
            outs["vp"].append(v_p.reshape(b, t, n_heads, dh))
            outs["fp"].append(lf_p[:, :, :n_heads])

            lf_new = lf_s[0, :, :n_heads].reshape(db, s, n_heads)

            def rows_head_query(x):
                x = x.reshape(db, s, n_heads, dh)
                return jnp.transpose(x, (0, 2, 1, 3)).reshape(db, n_heads * s, dh)

            lf_col = jnp.transpose(lf_new, (0, 2, 1)).reshape(db, n_heads * s, 1)
            lf_col = jnp.broadcast_to(lf_col, (db, n_heads * s, LANES))
            lf_row = jnp.broadcast_to(lf_new.reshape(db, 1, s * n_heads), (db, SUBLANES, s * n_heads))
            bias = _decode_bias(page_table, cache_logf[j].reshape(n_phys, page * n_heads // LANES, LANES), n_heads)
            a_s = _decode_attention(
                page_table, rows_head_query(q_s),
                cache_k[j].reshape(n_phys, page * n_heads, dh),
                cache_v[j].reshape(n_phys, page * n_heads, dh),
                bias, k_s.reshape(db, s * n_heads, dh), v_s.reshape(db, s * n_heads, dh),
                lf_col, lf_row, rows_head_query(sg_s), n_heads, s)
            a_s = jnp.transpose(a_s.reshape(db, n_heads, s, dh), (0, 2, 1, 3)).reshape(1, rs, da)
            outs["ks"].append(k_s.reshape(db, s, n_heads, dh))
            outs["vs"].append(v_s.reshape(db, s, n_heads, dh))
            outs["fs"].append(lf_new)

            xp, xs = residual_out(a_p, a_s, w_out_attn, j, xp, xs, gate_p, gate_s, "out_attn")
        else:
            (sg_p,), (sg_s,) = _proj(hp, hs, w_in_conv, j, [2 * dc], dc, (_epi_silu, _epi_silu), [], [],
                                     *_plain_outs(b, t, rs, dc, BF16), "proj_gate")

            (u_p,), (u_s,) = _proj(hp, hs, w_in_conv, j, [0, dc], dc, (_epi_glu, _epi_glu), [], [],
                                   *_plain_outs(b, t, rs, dc, F32, tn=COL_TILE // 2), "proj_glu", tn=COL_TILE // 2)

            def conv_branch(u, sg, hist):
                rows = u.shape[1]
                pad_rows = -rows % SUBLANES
                u = jnp.pad(u, ((0, 0), (0, pad_rows), (0, 0)))
                sg = jnp.pad(sg, ((0, 0), (0, pad_rows), (0, 0)))
                hist = jnp.pad(hist, ((0, 0), (HIST_ROWS - conv_state, 0), (0, 0)))
                conv = _depthwise_conv(u, hist, dw_w[j], dw_b[j])
                return _ln_gate(conv, sg, ln_g[j], ln_b[j])[:, :rows]

            a_p = conv_branch(u_p, sg_p, jnp.zeros((b, conv_state, dc), F32))
            outs["cp"].append(u_p[:, t - conv_state:, :])
            u_s = u_s.reshape(db, s, dc)
            a_s = conv_branch(u_s, sg_s.reshape(db, s, dc), state_conv[j]).reshape(1, rs, dc)
            outs["cs"].append(jnp.concatenate([state_conv[j], u_s], axis=1)[:, -conv_state:, :])

            xp, xs = residual_out(a_p, a_s, w_out_conv, j, xp, xs, gate_p, gate_s, "out_conv")

    return (xp, xs.reshape(db, s, d), jnp.stack(outs["kp"]), jnp.stack(outs["vp"]), jnp.stack(outs["fp"]),
            jnp.stack(outs["ks"]), jnp.stack(outs["vs"]), jnp.stack(outs["fs"]),
            jnp.stack(outs["cp"]), jnp.stack(outs["cs"]))
```
